```python
import jax
import jax.numpy as jnp
from jax import lax
import numpy as np

D_MODEL = 2048
BATCH = 4
SEQ = 4096
DEPTH = 2

CHUNK = 64
MIX_WIDTH = 2 * D_MODEL
POOL_WIDTH = MIX_WIDTH // 4
N_POOL_GROUPS = 4
POOL_GROUP_DIM = POOL_WIDTH // N_POOL_GROUPS
POOL_WINDOWS = (2, 4, 8, 16)
SB_WIDTH = MIX_WIDTH // 2
SB_HEAD_DIM = 128
SB_HEADS = SB_WIDTH // SB_HEAD_DIM
Q_BLOCK = 128
CONV_CH = MIX_WIDTH // 4
CONV_WIDTH = 3
N_BRANCHES = 3
IN_PROJ_DIM = 2 * POOL_WIDTH + 4 * SB_WIDTH + 4 * CONV_CH
RMS_EPS = 1e-6

kernel_name = "hybrid_pool_stickbreak_shortconv_trunk"


def rms_norm(x, g):
    xf = x.astype(jnp.float32)
    y = xf * lax.rsqrt(jnp.mean(xf * xf, axis=-1, keepdims=True) + RMS_EPS)
    return (y * g.astype(jnp.float32)).astype(x.dtype)


def split_input_projection(proj):
    sizes = (POOL_WIDTH, POOL_WIDTH,
             SB_WIDTH, SB_WIDTH, SB_WIDTH, SB_WIDTH,
             CONV_CH, CONV_CH, CONV_CH, CONV_CH)
    parts, off = [], 0
    for n in sizes:
        parts.append(proj[..., off:off + n])
        off += n
    return parts


def multiscale_pool(xa, pool_w, pool_scale):
    b, s, _ = xa.shape
    xg = xa.astype(jnp.float32).reshape(b, s, N_POOL_GROUPS, POOL_GROUP_DIM)
    cs = jnp.pad(jnp.cumsum(xg, axis=1), ((0, 0), (1, 0), (0, 0), (0, 0)))
    t = jnp.arange(s)
    pooled = []
    for g, w in enumerate(POOL_WINDOWS):
        start = jnp.maximum(t + 1 - w, 0)
        count = (t + 1 - start).astype(jnp.float32)
        win_sum = cs[:, 1:, g] - cs[:, start, g]
        pooled.append(win_sum / count[None, :, None])
    mixed = jnp.stack(pooled, axis=2) - xg
    y = jnp.einsum('bsgc,gcd->bsgd', mixed, pool_w.astype(jnp.float32))
    return (y.reshape(b, s, POOL_WIDTH) * pool_scale).astype(xa.dtype)


def stick_breaking_attention(q, k, v):
    b, s, h, dh = q.shape
    qf = q.astype(jnp.float32) * (dh ** -0.5)
    kf = k.astype(jnp.float32)
    vf = v.astype(jnp.float32)
    outs = []
    for i in range(s // Q_BLOCK):
        q0 = i * Q_BLOCK
        kv_len = q0 + Q_BLOCK
        z = jnp.einsum('bqhd,bkhd->bhqk', qf[:, q0:kv_len], kf[:, :kv_len])
        t_idx = q0 + jnp.arange(Q_BLOCK)[:, None]
        s_idx = jnp.arange(kv_len)[None, :]
        strict = s_idx < t_idx
        log_keep = jnp.where(strict, -jax.nn.softplus(z), 0.0)
        between = lax.cumsum(log_keep, axis=3, reverse=True) - log_keep
        log_a = jax.nn.log_sigmoid(z) + between
        a = jnp.where(strict, jnp.exp(log_a), 0.0)
        outs.append(jnp.einsum('bhqk,bkhd->bqhd', a, vf[:, :kv_len]))
    return jnp.concatenate(outs, axis=1).astype(q.dtype)


def short_gated_conv(u, b_gate, c_gate, conv_w):
    s = u.shape[1]
    v = c_gate * u
    vp = jnp.pad(v, ((0, 0), (CONV_WIDTH - 1, 0), (0, 0)))
    y = conv_w[0] * vp[:, 0:s]
    for i in range(1, CONV_WIDTH):
        y = y + conv_w[i] * vp[:, i:i + s]
    return b_gate * y


def hybrid_layer(x, c, norm_g, w_ada, b_ada, w_in, pool_w, pool_scale, conv_w,
                 w_br_a, w_br_b, w_br_c, w_gate, b_gate, w_out):
    b, s, _ = x.shape
    mod = jax.nn.silu(c) @ w_ada + b_ada
    shift, scale, res_gate = jnp.split(mod, 3, axis=-1)
    h = rms_norm(x, norm_g) * (1.0 + scale[:, None, :]) + shift[:, None, :]
    xa, za, q, k, v, zb, u, bg, cg, zc = split_input_projection(h @ w_in)
    ya = multiscale_pool(xa, pool_w, pool_scale) * jax.nn.silu(za)
    hs = (b, s, SB_HEADS, SB_HEAD_DIM)
    yb = stick_breaking_attention(q.reshape(hs), k.reshape(hs), v.reshape(hs))
    yb = yb.reshape(b, s, SB_WIDTH) * jax.nn.silu(zb)
    yc = short_gated_conv(u, bg, cg, conv_w) * jax.nn.silu(zc)
    gates = jax.nn.sigmoid((h @ w_gate + b_gate).astype(jnp.float32)).astype(x.dtype)
    gates = gates.reshape(b, s, N_BRANCHES, D_MODEL)
    merged = (gates[:, :, 0] * (ya @ w_br_a)
              + gates[:, :, 1] * (yb @ w_br_b)
              + gates[:, :, 2] * (yc @ w_br_c))
    return x + res_gate[:, None, :] * (merged @ w_out)


def setup_inputs(seed: int = 0) -> dict:
    key = jax.random.key(seed)
    ks = jax.random.split(key, 18)

    def nrm(k, shape, scale):
        return scale * jax.random.normal(k, shape, jnp.float32)

    return {
        "x": nrm(ks[0], (BATCH, SEQ, D_MODEL), 1.0),
        "c": nrm(ks[1], (BATCH, D_MODEL), 1.0),
        "norm_g": 1.0 + nrm(ks[2], (DEPTH, D_MODEL), 0.1),
        "w_ada": nrm(ks[3], (DEPTH, D_MODEL, 3 * D_MODEL), D_MODEL ** -0.5),
        "b_ada": nrm(ks[4], (DEPTH, 3 * D_MODEL), 0.02),
        "w_in": nrm(ks[5], (DEPTH, D_MODEL, IN_PROJ_DIM), D_MODEL ** -0.5),
        "pool_w": nrm(ks[6], (DEPTH, N_POOL_GROUPS, POOL_GROUP_DIM, POOL_GROUP_DIM), POOL_GROUP_DIM ** -0.5),
        "pool_scale": 1.0 + nrm(ks[7], (DEPTH, POOL_WIDTH), 0.1),
        "conv_w": nrm(ks[8], (DEPTH, CONV_WIDTH, CONV_CH), CONV_WIDTH ** -0.5),
        "w_br_a": nrm(ks[9], (DEPTH, POOL_WIDTH, D_MODEL), POOL_WIDTH ** -0.5),
        "w_br_b": nrm(ks[10], (DEPTH, SB_WIDTH, D_MODEL), SB_WIDTH ** -0.5),
        "w_br_c": nrm(ks[11], (DEPTH, CONV_CH, D_MODEL), CONV_CH ** -0.5),
        "w_gate": nrm(ks[12], (DEPTH, D_MODEL, N_BRANCHES * D_MODEL), D_MODEL ** -0.5),
        "b_gate": nrm(ks[13], (DEPTH, N_BRANCHES * D_MODEL), 0.1),
        "w_out": nrm(ks[14], (DEPTH, D_MODEL, D_MODEL), D_MODEL ** -0.5),
        "final_g": 1.0 + nrm(ks[15], (D_MODEL,), 0.1),
    }


def reference(x, c, norm_g, w_ada, b_ada, w_in, pool_w, pool_scale, conv_w,
              w_br_a, w_br_b, w_br_c, w_gate, b_gate, w_out, final_g):
    for l in range(DEPTH):
        x = hybrid_layer(x, c, norm_g[l], w_ada[l], b_ada[l], w_in[l], pool_w[l],
                         pool_scale[l], conv_w[l], w_br_a[l], w_br_b[l], w_br_c[l],
                         w_gate[l], b_gate[l], w_out[l])
    return rms_norm(x, final_g)
```

```python
import functools

import jax
import jax.numpy as jnp
from jax import lax
from jax.experimental import pallas as pl
from jax.experimental.pallas import tpu as pltpu

RMS_EPS = 1e-6
HEAD_DIM = 128
POOL_WINDOWS = (2, 4, 8, 16)
CONV_WIDTH = 3
HALO = 16
VMEM_LIMIT_BYTES = 56 * 1024 * 1024

F32 = jnp.float32
BF16 = jnp.bfloat16


def _params(*sem):
    return pltpu.CompilerParams(dimension_semantics=sem, vmem_limit_bytes=VMEM_LIMIT_BYTES)


def _silu(v):
    return v * jax.nn.sigmoid(v)


def _ada_kernel(c_ref, w_ref, b_ref, o_ref):
    s = _silu(c_ref[...])
    o_ref[0] = jnp.dot(s, w_ref[0], precision=lax.Precision.HIGHEST,
                       preferred_element_type=F32) + b_ref[0]


def _ada(c_pad, w_ada, b_ada, tn=768):
    depth, d, n3 = w_ada.shape
    rows = c_pad.shape[0]
    return pl.pallas_call(
        _ada_kernel,
        grid=(depth, n3 // tn),
        in_specs=[
            pl.BlockSpec((rows, d), lambda l, j: (0, 0)),
            pl.BlockSpec((1, d, tn), lambda l, j: (l, 0, j)),
            pl.BlockSpec((1, 1, tn), lambda l, j: (l, 0, j)),
        ],
        out_specs=pl.BlockSpec((1, rows, tn), lambda l, j: (l, 0, j)),
        out_shape=jax.ShapeDtypeStruct((depth, rows, n3), F32),
        compiler_params=_params("parallel", "parallel"),
        name="ada_mod",
    )(c_pad, w_ada, b_ada.reshape(depth, 1, n3))


def _inproj_kernel(x_ref, g_ref, sc_ref, sh_ref, w_ref, *rest, gated, chunk):
    if gated:
        b_ref, o_ref, h_ref = rest
    else:
        o_ref, h_ref = rest

    @pl.when(pl.program_id(1) == 0)
    def _():
        g = g_ref[...]
        sc = 1.0 + sc_ref[0]
        sh = sh_ref[0]

        def body(r, carry):
            rows = pl.ds(pl.multiple_of(r * chunk, chunk), chunk)
            xc = x_ref[rows, :]
            ms = jnp.mean(xc * xc, axis=-1, keepdims=True)
            y = xc * lax.rsqrt(ms + RMS_EPS) * g
            h_ref[rows, :] = (y * sc + sh).astype(BF16)
            return carry

        lax.fori_loop(0, x_ref.shape[0] // chunk, body, 0)

    acc = jnp.dot(h_ref[...], w_ref[...], preferred_element_type=F32)
    if gated:
        acc = jax.nn.sigmoid(acc + b_ref[...])
    o_ref[...] = acc.astype(o_ref.dtype)


def _inproj(x2, g, scale3, shift3, w, bias, seq, tm, tn):
    n, d = x2.shape
    ncol = w.shape[1]
    per_seq = seq // tm
    gated = bias is not None
    in_specs = [
        pl.BlockSpec((tm, d), lambda i, j: (i, 0)),
        pl.BlockSpec((1, d), lambda i, j: (0, 0)),
        pl.BlockSpec((1, 1, d), lambda i, j: (i // per_seq, 0, 0)),
        pl.BlockSpec((1, 1, d), lambda i, j: (i // per_seq, 0, 0)),
        pl.BlockSpec((d, tn), lambda i, j: (0, j)),
    ]
    args = [x2, g, scale3, shift3, w]
    if gated:
        in_specs.append(pl.BlockSpec((1, tn), lambda i, j: (0, j)))
        args.append(bias)
    return pl.pallas_call(
        functools.partial(_inproj_kernel, gated=gated, chunk=64),
        grid=(n // tm, ncol // tn),
        in_specs=in_specs,
        out_specs=pl.BlockSpec((tm, tn), lambda i, j: (i, j)),
        out_shape=jax.ShapeDtypeStruct((n, ncol), BF16),
        scratch_shapes=[pltpu.VMEM((tm, d), BF16)],
        compiler_params=_params("parallel", "arbitrary"),
        name="gate_proj" if gated else "in_proj",
    )(*args)


def _softplus(z):
    return jnp.maximum(z, 0.0) + jnp.log1p(jnp.exp(-jnp.abs(z)))


def _attn_kernel(q_ref, k_ref, v_ref, zb_ref, o_ref, *, t):
    seq = q_ref.shape[0]
    scale = HEAD_DIM ** -0.5
    row = lax.broadcasted_iota(jnp.int32, (t, t), 0)
    col = lax.broadcasted_iota(jnp.int32, (t, t), 1)
    strict = col < row
    suffix = (row > col).astype(BF16)

    def block(q, j, carry, acc, masked):
        ks = pl.ds(pl.multiple_of(j * t, t), t)
        z = lax.dot_general(q, k_ref[ks, :], (((1,), (1,)), ((), ())),
                            preferred_element_type=F32) * scale
        sp = _softplus(z)
        logsig = z - sp
        if masked:
            sp = jnp.where(strict, sp, 0.0)
        sp_hi = sp.astype(BF16)
        sp_lo = (sp - sp_hi.astype(F32)).astype(BF16)
        between = (jnp.dot(sp_hi, suffix, preferred_element_type=F32)
                   + jnp.dot(sp_lo, suffix, preferred_element_type=F32))
        a = jnp.exp(logsig - between)
        if masked:
            a = jnp.where(strict, a, 0.0)
        pv = jnp.dot(a.astype(BF16), v_ref[ks, :], preferred_element_type=F32)
        acc = acc + jnp.exp(-carry) * pv
        carry = carry + jnp.sum(sp, axis=1, keepdims=True)
        return carry, acc

    def q_block(i, _):
        qs = pl.ds(pl.multiple_of(i * t, t), t)
        q = q_ref[qs, :]
        carry = jnp.zeros((t, 1), F32)
        acc = jnp.zeros((t, HEAD_DIM), F32)
        carry, acc = block(q, i, carry, acc, True)

        def kv_step(n, state):
            return block(q, i - 1 - n, state[0], state[1], False)

        carry, acc = lax.fori_loop(0, i, kv_step, (carry, acc))
        o_ref[qs, :] = (acc * _silu(zb_ref[qs, :].astype(F32))).astype(o_ref.dtype)
        return 0

    lax.fori_loop(0, seq // t, q_block, 0)


def _attention(proj, batch, seq, sb_width, q_off, t=256):
    n = proj.shape[0]
    heads = sb_width // HEAD_DIM
    hb = sb_width // HEAD_DIM
    c0 = q_off // HEAD_DIM

    def spec(section):
        return pl.BlockSpec((seq, HEAD_DIM), lambda b, h: (b, c0 + section * hb + h))

    return pl.pallas_call(
        functools.partial(_attn_kernel, t=t),
        grid=(batch, heads),
        in_specs=[spec(0), spec(1), spec(2), spec(3)],
        out_specs=pl.BlockSpec((seq, HEAD_DIM), lambda b, h: (b, h)),
        out_shape=jax.ShapeDtypeStruct((n, sb_width), BF16),
        compiler_params=_params("parallel", "parallel"),
        name="stickbreak_attn",
    )(proj, proj, proj, proj)


def _mix_kernel(xz_ref, ub_ref, cz_ref, xa_h_ref, u_h_ref, cg_h_ref, yb_ref,
                pw_ref, ps_ref, cw_ref, wa_ref, wb_ref, wc_ref, g0_ref, g1_ref, g2_ref,
                o_ref, ya_ref, yc_ref, *, per_seq):
    tm = xz_ref.shape[0]
    pool_width = xz_ref.shape[1] // 2
    conv_ch = ub_ref.shape[1] // 2
    group = pool_width // len(POOL_WINDOWS)
    i = pl.program_id(0)

    @pl.when(pl.program_id(1) == 0)
    def _():
        seq_start = (i % per_seq) == 0
        keep = jnp.where(seq_start, 0.0, 1.0)
        pos = (i % per_seq) * tm + lax.broadcasted_iota(jnp.int32, (tm, 1), 0)

        for g, w in enumerate(POOL_WINDOWS):
            cs = slice(g * group, (g + 1) * group)
            xa = xz_ref[:, cs].astype(F32)
            halo = xa_h_ref[:, cs].astype(F32) * keep
            s = jnp.concatenate([halo, xa], axis=0)
            shift = 1
            while shift < w:
                s = s + pltpu.roll(s, shift, axis=0)
                shift *= 2
            count = jnp.minimum(pos + 1, w).astype(F32)
            mixed = s[HALO:, :] / count - xa
            y = jnp.dot(mixed.astype(BF16), pw_ref[g], preferred_element_type=F32)
            za = xz_ref[:, pool_width + g * group: pool_width + (g + 1) * group].astype(F32)
            ya_ref[:, cs] = (y * ps_ref[:, cs] * _silu(za)).astype(BF16)

        u = ub_ref[:, :conv_ch].astype(F32)
        bg = ub_ref[:, conv_ch:].astype(F32)
        cg = cz_ref[:, :conv_ch].astype(F32)
        zc = cz_ref[:, conv_ch:].astype(F32)
        v = cg * u
        vh = cg_h_ref[...].astype(F32) * u_h_ref[...].astype(F32) * keep
        ve = jnp.concatenate([vh, v], axis=0)
        y = cw_ref[CONV_WIDTH - 1: CONV_WIDTH, :] * v
        for tap in range(CONV_WIDTH - 1):
            back = CONV_WIDTH - 1 - tap
            y = y + cw_ref[tap: tap + 1, :] * pltpu.roll(ve, back, axis=0)[HALO:, :]
        yc_ref[...] = (bg * y * _silu(zc)).astype(BF16)

    a = jnp.dot(ya_ref[...], wa_ref[...], preferred_element_type=F32)
    b = jnp.dot(yb_ref[...], wb_ref[...], preferred_element_type=F32)
    c = jnp.dot(yc_ref[...], wc_ref[...], preferred_element_type=F32)
    merged = (g0_ref[...].astype(F32) * a + g1_ref[...].astype(F32) * b
              + g2_ref[...].astype(F32) * c)
    o_ref[...] = merged.astype(o_ref.dtype)


def _mix(proj, yb, gates, pool_w, pool_scale, conv_w, w_a, w_b, w_c, seq, tm, tn):
    n = proj.shape[0]
    pool_width, d = w_a.shape
    sb_width = w_b.shape[0]
    conv_ch = w_c.shape[0]
    per_seq = seq // tm
    u_off = 2 * pool_width + 4 * sb_width
    nj = d // tn
    wide = 2 * pool_width
    assert 2 * conv_ch == wide and u_off % wide == 0

    def halo_rows(i, j):
        return jnp.maximum(i * (tm // HALO) - 1, 0)

    in_specs = [
        pl.BlockSpec((tm, wide), lambda i, j: (i, 0)),
        pl.BlockSpec((tm, wide), lambda i, j: (i, u_off // wide)),
        pl.BlockSpec((tm, wide), lambda i, j: (i, u_off // wide + 1)),
        pl.BlockSpec((HALO, pool_width), lambda i, j: (halo_rows(i, j), 0)),
        pl.BlockSpec((HALO, conv_ch), lambda i, j: (halo_rows(i, j), u_off // conv_ch)),
        pl.BlockSpec((HALO, conv_ch), lambda i, j: (halo_rows(i, j), u_off // conv_ch + 2)),
        pl.BlockSpec((tm, sb_width), lambda i, j: (i, 0)),
        pl.BlockSpec(pool_w.shape, lambda i, j: (0, 0, 0)),
        pl.BlockSpec((1, pool_width), lambda i, j: (0, 0)),
        pl.BlockSpec(conv_w.shape, lambda i, j: (0, 0)),
        pl.BlockSpec((pool_width, tn), lambda i, j: (0, j)),
        pl.BlockSpec((sb_width, tn), lambda i, j: (0, j)),
        pl.BlockSpec((conv_ch, tn), lambda i, j: (0, j)),
        pl.BlockSpec((tm, tn), lambda i, j: (i, j)),
        pl.BlockSpec((tm, tn), lambda i, j: (i, nj + j)),
        pl.BlockSpec((tm, tn), lambda i, j: (i, 2 * nj + j)),
    ]
    return pl.pallas_call(
        functools.partial(_mix_kernel, per_seq=per_seq),
        grid=(n // tm, nj),
        in_specs=in_specs,
        out_specs=pl.BlockSpec((tm, tn), lambda i, j: (i, j)),
        out_shape=jax.ShapeDtypeStruct((n, d), BF16),
        scratch_shapes=[pltpu.VMEM((tm, pool_width), BF16), pltpu.VMEM((tm, conv_ch), BF16)],
        compiler_params=_params("parallel", "arbitrary"),
        name="mix_merge",
    )(proj, proj, proj, proj, proj, proj, yb, pool_w, pool_scale, conv_w,
      w_a, w_b, w_c, gates, gates, gates)


def _out_kernel(m_ref, w_ref, x_ref, rg_ref, *rest, final):
    if final:
        fg_ref, o_ref = rest
    else:
        (o_ref,) = rest
    y = jnp.dot(m_ref[...], w_ref[...], preferred_element_type=F32)
    xn = x_ref[...] + rg_ref[0] * y
    if final:
        ms = jnp.mean(xn * xn, axis=-1, keepdims=True)
        xn = xn * lax.rsqrt(ms + RMS_EPS) * fg_ref[...]
    o_ref[...] = xn


def _out(merged, w_out, x2, res_gate3, final_g, seq, tm):
    n, d = x2.shape
    per_seq = seq // tm
    final = final_g is not None
    in_specs = [
        pl.BlockSpec((tm, d), lambda i: (i, 0)),
        pl.BlockSpec((d, d), lambda i: (0, 0)),
        pl.BlockSpec((tm, d), lambda i: (i, 0)),
        pl.BlockSpec((1, 1, d), lambda i: (i // per_seq, 0, 0)),
    ]
    args = [merged, w_out, x2, res_gate3]
    if final:
        in_specs.append(pl.BlockSpec((1, d), lambda i: (0, 0)))
        args.append(final_g)
    return pl.pallas_call(
        functools.partial(_out_kernel, final=final),
        grid=(n // tm,),
        in_specs=in_specs,
        out_specs=pl.BlockSpec((tm, d), lambda i: (i, 0)),
        out_shape=jax.ShapeDtypeStruct((n, d), F32),
        compiler_params=_params("parallel"),
        name="out_proj_final" if final else "out_proj",
    )(*args)


def kernel(x, c, norm_g, w_ada, b_ada, w_in, pool_w, pool_scale, conv_w, w_br_a, w_br_b,
           w_br_c, w_gate, b_gate, w_out, final_g):
    batch, seq, d = x.shape
    depth = w_in.shape[0]
    pool_width = w_br_a.shape[1]
    sb_width = w_br_b.shape[1]
    n = batch * seq
    tm = min(512, seq)

    c_pad = jnp.pad(c, ((0, -batch % 8), (0, 0)))
    mod = _ada(c_pad, w_ada, b_ada)[:, :batch]
    x2 = x.reshape(n, d)

    for l in range(depth):
        shift3 = mod[l, :, :d].reshape(batch, 1, d)
        scale3 = mod[l, :, d:2 * d].reshape(batch, 1, d)
        res_gate3 = mod[l, :, 2 * d:].reshape(batch, 1, d)
        g = norm_g[l].reshape(1, d)

        proj = _inproj(x2, g, scale3, shift3, w_in[l].astype(BF16), None, seq, tm, 512)
        gates = _inproj(x2, g, scale3, shift3, w_gate[l].astype(BF16),
                        b_gate[l].reshape(1, -1), seq, tm, 512)
        yb = _attention(proj, batch, seq, sb_width, 2 * pool_width)
        merged = _mix(proj, yb, gates, pool_w[l].astype(BF16), pool_scale[l].reshape(1, -1),
                      conv_w[l], w_br_a[l].astype(BF16), w_br_b[l].astype(BF16),
                      w_br_c[l].astype(BF16), seq, tm, 512)
        x2 = _out(merged, w_out[l].astype(BF16), x2, res_gate3,
                  final_g.reshape(1, d) if l == depth - 1 else None, seq, tm)

    return x2.reshape(batch, seq, d)
```

```python
import functools

import jax
import jax.numpy as jnp
from jax import lax
from jax.experimental import pallas as pl
from jax.experimental.pallas import tpu as pltpu

RMS_EPS = 1e-6
HEAD_DIM = 128
POOL_WINDOWS = (2, 4, 8, 16)
CONV_WIDTH = 3
HALO = 16
VMEM_LIMIT_BYTES = 56 * 1024 * 1024

F32 = jnp.float32
BF16 = jnp.bfloat16


def _params(*sem):
    return pltpu.CompilerParams(dimension_semantics=sem, vmem_limit_bytes=VMEM_LIMIT_BYTES)


def _silu(v):
    return v * jax.nn.sigmoid(v)


def _ada_kernel(c_ref, w_ref, b_ref, o_ref):
    s = _silu(c_ref[...])
    o_ref[0] = jnp.dot(s, w_ref[0], precision=lax.Precision.HIGHEST,
                       preferred_element_type=F32) + b_ref[0]


def _ada(c_pad, w_ada, b_ada, tn=768):
    depth, d, n3 = w_ada.shape
    rows = c_pad.shape[0]
    return pl.pallas_call(
        _ada_kernel,
        grid=(depth, n3 // tn),
        in_specs=[
            pl.BlockSpec((rows, d), lambda l, j: (0, 0)),
            pl.BlockSpec((1, d, tn), lambda l, j: (l, 0, j)),
            pl.BlockSpec((1, 1, tn), lambda l, j: (l, 0, j)),
        ],
        out_specs=pl.BlockSpec((1, rows, tn), lambda l, j: (l, 0, j)),
        out_shape=jax.ShapeDtypeStruct((depth, rows, n3), F32),
        compiler_params=_params("parallel", "parallel"),
        name="ada_mod",
    )(c_pad, w_ada, b_ada.reshape(depth, 1, n3))


def _inproj_kernel(x_ref, g_ref, sc_ref, sh_ref, w_ref, b_ref, o_ref, h_ref, *, n_plain, chunk):
    j = pl.program_id(1)

    @pl.when(j == 0)
    def _():
        g = g_ref[...]
        sc = 1.0 + sc_ref[0]
        sh = sh_ref[0]

        def body(r, carry):
            rows = pl.ds(pl.multiple_of(r * chunk, chunk), chunk)
            xc = x_ref[rows, :]
            ms = jnp.mean(xc * xc, axis=-1, keepdims=True)
            y = xc * lax.rsqrt(ms + RMS_EPS) * g
            h_ref[rows, :] = (y * sc + sh).astype(BF16)
            return carry

        lax.fori_loop(0, x_ref.shape[0] // chunk, body, 0)

    @pl.when(j < n_plain)
    def _():
        acc = jnp.dot(h_ref[...], w_ref[...], preferred_element_type=F32)
        o_ref[...] = acc.astype(o_ref.dtype)

    @pl.when(j >= n_plain)
    def _():
        acc = jnp.dot(h_ref[...], w_ref[...], preferred_element_type=F32)
        o_ref[...] = jax.nn.sigmoid(acc + b_ref[...]).astype(o_ref.dtype)


def _inproj(x2, g, scale3, shift3, w_all, b_gate, seq, tm, tn):
    n, d = x2.shape
    ncol = w_all.shape[1]
    n_plain = (ncol - b_gate.shape[1]) // tn
    per_seq = seq // tm
    return pl.pallas_call(
        functools.partial(_inproj_kernel, n_plain=n_plain, chunk=64),
        grid=(n // tm, ncol // tn),
        in_specs=[
            pl.BlockSpec((tm, d), lambda i, j: (i, 0)),
            pl.BlockSpec((1, d), lambda i, j: (0, 0)),
            pl.BlockSpec((1, 1, d), lambda i, j: (i // per_seq, 0, 0)),
            pl.BlockSpec((1, 1, d), lambda i, j: (i // per_seq, 0, 0)),
            pl.BlockSpec((d, tn), lambda i, j: (0, j)),
            pl.BlockSpec((1, tn), lambda i, j: (0, jnp.maximum(j - n_plain, 0))),
        ],
        out_specs=pl.BlockSpec((tm, tn), lambda i, j: (i, j)),
        out_shape=jax.ShapeDtypeStruct((n, ncol), BF16),
        scratch_shapes=[pltpu.VMEM((tm, d), BF16)],
        compiler_params=_params("parallel", "arbitrary"),
        name="in_proj",
    )(x2, g, scale3, shift3, w_all, b_gate)


LOG2E = 1.4426950408889634
DEAD_CARRY = 105.0
MASKED_LOGIT = -1e30


def _attn_kernel(q_ref, k_ref, v_ref, zb_ref, o_ref, *, rows, unroll):
    seq = q_ref.shape[0]
    win = 2 * rows
    ahead = (lax.broadcasted_iota(jnp.int32, (rows, win), 1)
             - lax.broadcasted_iota(jnp.int32, (rows, win), 0))
    suffix = (lax.broadcasted_iota(jnp.int32, (win, win), 0)
              > lax.broadcasted_iota(jnp.int32, (win, win), 1)).astype(BF16)

    def scores(q, k_blk, visible):
        z = lax.dot_general(q, k_blk, (((1,), (1,)), ((), ())), preferred_element_type=F32)
        if visible is not None:
            z = jnp.where(visible, z, MASKED_LOGIT)
        return z, jnp.maximum(z, 0.0) + jnp.log(1.0 + jnp.exp2(jnp.abs(z) * -LOG2E))

    def later_sums(sp, tri):
        sp_hi = sp.astype(BF16)
        sp_lo = (sp - sp_hi.astype(F32)).astype(BF16)
        return (jnp.dot(sp_hi, tri, preferred_element_type=F32)
                + jnp.dot(sp_lo, tri, preferred_element_type=F32))

    def weighted(z, sp, between, v_blk):
        a = jnp.exp2((z - sp - between) * LOG2E)
        return jnp.dot(a.astype(BF16), v_blk, preferred_element_type=F32)

    def group_rows(g):
        return pl.ds(pl.multiple_of(g * rows, rows), rows)

    def step(it, _):
        groups = [it * unroll + u for u in range(unroll)]
        starts = [pl.multiple_of(jnp.maximum(g - 1, 0) * rows, rows) for g in groups]
        qs = [q_ref[group_rows(g), :] for g in groups]
        zs = [scores(q, k_ref[pl.ds(s, win), :], ahead < g * rows - s)
              for g, s, q in zip(groups, starts, qs)]
        betweens = [later_sums(sp, suffix) for _, sp in zs]
        accs = [weighted(z, sp, b, v_ref[pl.ds(s, win), :])
                for (z, sp), b, s in zip(zs, betweens, starts)]
        carries = [b[:, 0:1] + sp[:, 0:1] for (_, sp), b in zip(zs, betweens)]

        def sweep_earlier(accs):
            out = []
            for g, s, q, acc, carry in zip(groups, starts, qs, accs, carries):
                def live(state):
                    j, carry, _ = state
                    return jnp.logical_and(j >= 0, jnp.min(carry) < DEAD_CARRY)

                def earlier_block(state):
                    j, carry, acc = state
                    blk = group_rows(j)
                    z, sp = scores(q, k_ref[blk, :], None)
                    between = later_sums(sp, suffix[:rows, :rows])
                    pv = weighted(z, sp, between, v_ref[blk, :])
                    return (j - 1, carry + between[:, 0:1] + sp[:, 0:1],
                            acc + jnp.exp(-carry) * pv)

                out.append(lax.while_loop(live, earlier_block, (s // rows - 1, carry, acc))[2])
            return out

        least = functools.reduce(jnp.minimum, carries)
        accs = lax.cond(jnp.min(least) < DEAD_CARRY, sweep_earlier, lambda accs: accs, accs)
        for g, acc in zip(groups, accs):
            zb = zb_ref[group_rows(g), :].astype(F32)
            o_ref[group_rows(g), :] = (acc * _silu(zb)).astype(o_ref.dtype)
        return 0

    lax.fori_loop(0, seq // (rows * unroll), step, 0)


def _attention(pg, batch, seq, sb_width, q_off, rows=128, unroll=8):
    n = pg.shape[0]
    heads = sb_width // HEAD_DIM
    c0 = q_off // HEAD_DIM
    assert seq % (rows * unroll) == 0 and seq >= 2 * rows

    def spec(section):
        return pl.BlockSpec((seq, HEAD_DIM), lambda b, h: (b, c0 + section * heads + h))

    return pl.pallas_call(
        functools.partial(_attn_kernel, rows=rows, unroll=unroll),
        grid=(batch, heads),
        in_specs=[spec(0), spec(1), spec(2), spec(3)],
        out_specs=pl.BlockSpec((seq, HEAD_DIM), lambda b, h: (b, h)),
        out_shape=jax.ShapeDtypeStruct((n, sb_width), BF16),
        compiler_params=_params("parallel", "parallel"),
        name="stickbreak_attn",
    )(pg, pg, pg, pg)


def _mix_kernel(xz_ref, ub_ref, cz_ref, xa_h_ref, u_h_ref, cg_h_ref, yb_ref,
                pw_ref, ps_ref, cw_ref, wa_ref, wb_ref, wc_ref, g0_ref, g1_ref, g2_ref,
                o_ref, ya_ref, yc_ref, *, per_seq):
    tm = xz_ref.shape[0]
    pool_width = xz_ref.shape[1] // 2
    conv_ch = ub_ref.shape[1] // 2
    group = pool_width // len(POOL_WINDOWS)
    i = pl.program_id(0)

    b = jnp.dot(yb_ref[...], wb_ref[...], preferred_element_type=F32)

    seq_start = (i % per_seq) == 0
    keep = jnp.where(seq_start, 0.0, 1.0)
    pos = (i % per_seq) * tm + lax.broadcasted_iota(jnp.int32, (tm, 1), 0)

    for g, w in enumerate(POOL_WINDOWS):
        cs = slice(g * group, (g + 1) * group)
        xa = xz_ref[:, cs].astype(F32)
        halo = xa_h_ref[:, cs].astype(F32) * keep
        s = jnp.concatenate([halo, xa], axis=0)
        shift = 1
        while shift < w:
            s = s + pltpu.roll(s, shift, axis=0)
            shift *= 2
        count = jnp.minimum(pos + 1, w).astype(F32)
        mixed = s[HALO:, :] / count - xa
        y = jnp.dot(mixed.astype(BF16), pw_ref[g], preferred_element_type=F32)
        za = xz_ref[:, pool_width + g * group: pool_width + (g + 1) * group].astype(F32)
        ya_ref[:, cs] = (y * ps_ref[:, cs] * _silu(za)).astype(BF16)

    u = ub_ref[:, :conv_ch].astype(F32)
    bg = ub_ref[:, conv_ch:].astype(F32)
    cg = cz_ref[:, :conv_ch].astype(F32)
    zc = cz_ref[:, conv_ch:].astype(F32)
    v = cg * u
    vh = cg_h_ref[...].astype(F32) * u_h_ref[...].astype(F32) * keep
    ve = jnp.concatenate([vh, v], axis=0)
    y = cw_ref[CONV_WIDTH - 1: CONV_WIDTH, :] * v
    for tap in range(CONV_WIDTH - 1):
        back = CONV_WIDTH - 1 - tap
        y = y + cw_ref[tap: tap + 1, :] * pltpu.roll(ve, back, axis=0)[HALO:, :]
    yc_ref[...] = (bg * y * _silu(zc)).astype(BF16)

    a = jnp.dot(ya_ref[...], wa_ref[...], preferred_element_type=F32)
    c = jnp.dot(yc_ref[...], wc_ref[...], preferred_element_type=F32)
    merged = (g0_ref[...].astype(F32) * a + g1_ref[...].astype(F32) * b
              + g2_ref[...].astype(F32) * c)
    o_ref[...] = merged.astype(o_ref.dtype)


def _mix(pg, yb, pool_w, pool_scale, conv_w, w_a, w_b, w_c, seq, tm):
    n = pg.shape[0]
    pool_width, d = w_a.shape
    sb_width = w_b.shape[0]
    conv_ch = w_c.shape[0]
    per_seq = seq // tm
    u_off = 2 * pool_width + 4 * sb_width
    gate_off = u_off + 4 * conv_ch
    wide = 2 * pool_width
    assert 2 * conv_ch == wide and u_off % wide == 0 and gate_off % d == 0

    def halo_rows(i):
        return jnp.maximum(i * (tm // HALO) - 1, 0)

    def resident(shape):
        return pl.BlockSpec(shape, lambda i: (0,) * len(shape), pipeline_mode=pl.Buffered(1))

    in_specs = [
        pl.BlockSpec((tm, wide), lambda i: (i, 0)),
        pl.BlockSpec((tm, wide), lambda i: (i, u_off // wide)),
        pl.BlockSpec((tm, wide), lambda i: (i, u_off // wide + 1)),
        pl.BlockSpec((HALO, pool_width), lambda i: (halo_rows(i), 0)),
        pl.BlockSpec((HALO, conv_ch), lambda i: (halo_rows(i), u_off // conv_ch)),
        pl.BlockSpec((HALO, conv_ch), lambda i: (halo_rows(i), u_off // conv_ch + 2)),
        pl.BlockSpec((tm, sb_width), lambda i: (i, 0)),
        resident(pool_w.shape),
        resident((1, pool_width)),
        resident(conv_w.shape),
        resident(w_a.shape),
        resident(w_b.shape),
        resident(w_c.shape),
        pl.BlockSpec((tm, d), lambda i: (i, gate_off // d)),
        pl.BlockSpec((tm, d), lambda i: (i, gate_off // d + 1)),
        pl.BlockSpec((tm, d), lambda i: (i, gate_off // d + 2)),
    ]
    return pl.pallas_call(
        functools.partial(_mix_kernel, per_seq=per_seq),
        grid=(n // tm,),
        in_specs=in_specs,
        out_specs=pl.BlockSpec((tm, d), lambda i: (i, 0)),
        out_shape=jax.ShapeDtypeStruct((n, d), BF16),
        scratch_shapes=[pltpu.VMEM((tm, pool_width), BF16), pltpu.VMEM((tm, conv_ch), BF16)],
        compiler_params=_params("parallel"),
        name="mix_merge",
    )(pg, pg, pg, pg, pg, pg, yb, pool_w, pool_scale, conv_w, w_a, w_b, w_c, pg, pg, pg)


def _out_kernel(m_ref, w_ref, x_ref, rg_ref, *rest, final):
    if final:
        fg_ref, o_ref = rest
    else:
        (o_ref,) = rest
    y = jnp.dot(m_ref[...], w_ref[...], preferred_element_type=F32)
    xn = x_ref[...] + rg_ref[0] * y
    if final:
        ms = jnp.mean(xn * xn, axis=-1, keepdims=True)
        xn = xn * lax.rsqrt(ms + RMS_EPS) * fg_ref[...]
    o_ref[...] = xn


def _out(merged, w_out, x2, res_gate3, final_g, seq, tm):
    n, d = x2.shape
    per_seq = seq // tm
    final = final_g is not None
    in_specs = [
        pl.BlockSpec((tm, d), lambda i: (i, 0)),
        pl.BlockSpec((d, d), lambda i: (0, 0)),
        pl.BlockSpec((tm, d), lambda i: (i, 0)),
        pl.BlockSpec((1, 1, d), lambda i: (i // per_seq, 0, 0)),
    ]
    args = [merged, w_out, x2, res_gate3]
    if final:
        in_specs.append(pl.BlockSpec((1, d), lambda i: (0, 0)))
        args.append(final_g)
    return pl.pallas_call(
        functools.partial(_out_kernel, final=final),
        grid=(n // tm,),
        in_specs=in_specs,
        out_specs=pl.BlockSpec((tm, d), lambda i: (i, 0)),
        out_shape=jax.ShapeDtypeStruct((n, d), F32),
        compiler_params=_params("parallel"),
        name="out_proj_final" if final else "out_proj",
    )(*args)


def kernel(x, c, norm_g, w_ada, b_ada, w_in, pool_w, pool_scale, conv_w, w_br_a, w_br_b,
           w_br_c, w_gate, b_gate, w_out, final_g):
    batch, seq, d = x.shape
    depth = w_in.shape[0]
    pool_width = w_br_a.shape[1]
    sb_width = w_br_b.shape[1]
    q_off = 2 * pool_width
    n = batch * seq
    tm = min(512, seq)

    c_pad = jnp.pad(c, ((0, -batch % 8), (0, 0)))
    mod = _ada(c_pad, w_ada, b_ada)[:, :batch]
    x2 = x.reshape(n, d)

    for l in range(depth):
        shift3 = mod[l, :, :d].reshape(batch, 1, d)
        scale3 = mod[l, :, d:2 * d].reshape(batch, 1, d)
        res_gate3 = mod[l, :, 2 * d:].reshape(batch, 1, d)
        g = norm_g[l].reshape(1, d)

        w_q_scaled = w_in[l].at[:, q_off:q_off + sb_width].multiply(HEAD_DIM ** -0.5)
        w_all = jnp.concatenate([w_q_scaled, w_gate[l]], axis=1).astype(BF16)
        pg = _inproj(x2, g, scale3, shift3, w_all, b_gate[l].reshape(1, -1), seq, tm, 2048)
        yb = _attention(pg, batch, seq, sb_width, q_off)
        merged = _mix(pg, yb, pool_w[l].astype(BF16), pool_scale[l].reshape(1, -1),
                      conv_w[l], w_br_a[l].astype(BF16), w_br_b[l].astype(BF16),
                      w_br_c[l].astype(BF16), seq, min(256, seq))
        x2 = _out(merged, w_out[l].astype(BF16), x2, res_gate3,
                  final_g.reshape(1, d) if l == depth - 1 else None, seq, tm)

    return x2.reshape(batch, seq, d)
```

```python
import functools

import jax
import jax.numpy as jnp
from jax import lax
from jax.experimental import pallas as pl
from jax.experimental.pallas import tpu as pltpu

RMS_EPS = 1e-6
HEAD_DIM = 128
POOL_WINDOWS = (2, 4, 8, 16)
CONV_WIDTH = 3
HALO = 16
VMEM_LIMIT_BYTES = 56 * 1024 * 1024

F32 = jnp.float32
BF16 = jnp.bfloat16


def _params(*sem):
    return pltpu.CompilerParams(dimension_semantics=sem, vmem_limit_bytes=VMEM_LIMIT_BYTES)


def _silu(v):
    return v * jax.nn.sigmoid(v)


def _ada_kernel(c_ref, w_ref, b_ref, o_ref):
    s = _silu(c_ref[...])
    o_ref[0] = jnp.dot(s, w_ref[0], precision=lax.Precision.HIGHEST,
                       preferred_element_type=F32) + b_ref[0]


def _ada(c_pad, w_ada, b_ada, tn=768):
    depth, d, n3 = w_ada.shape
    rows = c_pad.shape[0]
    return pl.pallas_call(
        _ada_kernel,
        grid=(depth, n3 // tn),
        in_specs=[
            pl.BlockSpec((rows, d), lambda l, j: (0, 0)),
            pl.BlockSpec((1, d, tn), lambda l, j: (l, 0, j)),
            pl.BlockSpec((1, 1, tn), lambda l, j: (l, 0, j)),
        ],
        out_specs=pl.BlockSpec((1, rows, tn), lambda l, j: (l, 0, j)),
        out_shape=jax.ShapeDtypeStruct((depth, rows, n3), F32),
        compiler_params=_params("parallel", "parallel"),
        name="ada_mod",
    )(c_pad, w_ada, b_ada.reshape(depth, 1, n3))


def _inproj_kernel(x_ref, g_ref, sc_ref, sh_ref, w_ref, b_ref, o_ref, h_ref, *, n_plain, chunk):
    j = pl.program_id(1)

    @pl.when(j == 0)
    def _():
        g = g_ref[...]
        sc = 1.0 + sc_ref[0]
        sh = sh_ref[0]

        def body(r, carry):
            rows = pl.ds(pl.multiple_of(r * chunk, chunk), chunk)
            xc = x_ref[rows, :]
            ms = jnp.mean(xc * xc, axis=-1, keepdims=True)
            y = xc * lax.rsqrt(ms + RMS_EPS) * g
            h_ref[rows, :] = (y * sc + sh).astype(BF16)
            return carry

        lax.fori_loop(0, x_ref.shape[0] // chunk, body, 0)

    @pl.when(j < n_plain)
    def _():
        acc = jnp.dot(h_ref[...], w_ref[...], preferred_element_type=F32)
        o_ref[...] = acc.astype(o_ref.dtype)

    @pl.when(j >= n_plain)
    def _():
        acc = jnp.dot(h_ref[...], w_ref[...], preferred_element_type=F32)
        o_ref[...] = jax.nn.sigmoid(acc + b_ref[...]).astype(o_ref.dtype)


def _inproj(x2, g, scale3, shift3, w_all, b_gate, seq, tm, tn):
    n, d = x2.shape
    ncol = w_all.shape[1]
    n_plain = (ncol - b_gate.shape[1]) // tn
    per_seq = seq // tm
    return pl.pallas_call(
        functools.partial(_inproj_kernel, n_plain=n_plain, chunk=64),
        grid=(n // tm, ncol // tn),
        in_specs=[
            pl.BlockSpec((tm, d), lambda i, j: (i, 0)),
            pl.BlockSpec((1, d), lambda i, j: (0, 0)),
            pl.BlockSpec((1, 1, d), lambda i, j: (i // per_seq, 0, 0)),
            pl.BlockSpec((1, 1, d), lambda i, j: (i // per_seq, 0, 0)),
            pl.BlockSpec((d, tn), lambda i, j: (0, j)),
            pl.BlockSpec((1, tn), lambda i, j: (0, jnp.maximum(j - n_plain, 0))),
        ],
        out_specs=pl.BlockSpec((tm, tn), lambda i, j: (i, j)),
        out_shape=jax.ShapeDtypeStruct((n, ncol), BF16),
        scratch_shapes=[pltpu.VMEM((tm, d), BF16)],
        compiler_params=_params("parallel", "arbitrary"),
        name="in_proj",
    )(x2, g, scale3, shift3, w_all, b_gate)


LOG2E = 1.4426950408889634
DEAD_CARRY = 105.0
MASKED_LOGIT = -1e30


def _attn_kernel(q_ref, k_ref, v_ref, zb_ref, o_ref, *, rows, unroll):
    seq = q_ref.shape[0]
    win = 2 * rows
    col = lax.broadcasted_iota(jnp.int32, (rows, win), 1)
    ahead = col - lax.broadcasted_iota(jnp.int32, (rows, win), 0)
    suffix = (lax.broadcasted_iota(jnp.int32, (win, win), 0)
              > lax.broadcasted_iota(jnp.int32, (win, win), 1)).astype(BF16)

    def chunk_start(top):
        return pl.multiple_of(jnp.maximum(top - win, 0), rows)

    def scores(q, start, visible):
        z = lax.dot_general(q, k_ref[pl.ds(start, win), :], (((1,), (1,)), ((), ())),
                            preferred_element_type=F32)
        z = jnp.where(visible, z, MASKED_LOGIT)
        return z, jnp.maximum(z, 0.0) + jnp.log(1.0 + jnp.exp2(jnp.abs(z) * -LOG2E))

    def later_sums(sp):
        sp_hi = sp.astype(BF16)
        sp_lo = (sp - sp_hi.astype(F32)).astype(BF16)
        return (jnp.dot(sp_hi, suffix, preferred_element_type=F32)
                + jnp.dot(sp_lo, suffix, preferred_element_type=F32))

    def weighted(z, sp, between, start):
        a = jnp.exp2((z - sp - between) * LOG2E)
        return jnp.dot(a.astype(BF16), v_ref[pl.ds(start, win), :], preferred_element_type=F32)

    def row_sums(sp, between):
        return between[:, 0:1] + sp[:, 0:1]

    def group_rows(g):
        return pl.ds(pl.multiple_of(g * rows, rows), rows)

    def step(it, _):
        groups = [it * unroll + u for u in range(unroll)]
        qs = [q_ref[group_rows(g), :] for g in groups]
        starts1 = [chunk_start((g + 1) * rows) for g in groups]
        starts2 = [chunk_start(s1) for s1 in starts1]
        starts = starts1 + starts2
        visible = ([ahead < g * rows - s1 for g, s1 in zip(groups, starts1)]
                   + [col < s1 - s2 for s1, s2 in zip(starts1, starts2)])
        zs = [scores(q, s, vis) for q, s, vis in zip(qs + qs, starts, visible)]
        betweens = [later_sums(sp) for _, sp in zs]
        pvs = [weighted(z, sp, b, s) for (z, sp), b, s in zip(zs, betweens, starts)]
        sums = [row_sums(sp, b) for (_, sp), b in zip(zs, betweens)]

        accs = [pvs[u] + jnp.exp(-sums[u]) * pvs[unroll + u] for u in range(unroll)]
        carries = [sums[u] + sums[unroll + u] for u in range(unroll)]
        alive = [jnp.min(carry) < DEAD_CARRY for carry in carries]

        for u, g in enumerate(groups):
            def sweep_earlier(q=qs[u], acc=accs[u], carry=carries[u], top=starts2[u]):
                def live(state):
                    top, carry, _ = state
                    return jnp.logical_and(top > 0, jnp.min(carry) < DEAD_CARRY)

                def earlier_chunk(state):
                    top, carry, acc = state
                    start = chunk_start(top)
                    z, sp = scores(q, start, col < top - start)
                    between = later_sums(sp)
                    pv = weighted(z, sp, between, start)
                    return start, carry + row_sums(sp, between), acc + jnp.exp(-carry) * pv

                return lax.while_loop(live, earlier_chunk, (top, carry, acc))[2]

            acc = lax.cond(alive[u], sweep_earlier, lambda acc=accs[u]: acc)
            zb = zb_ref[group_rows(g), :].astype(F32)
            o_ref[group_rows(g), :] = (acc * _silu(zb)).astype(o_ref.dtype)
        return 0

    lax.fori_loop(0, seq // (rows * unroll), step, 0)


def _attention(pg, batch, seq, sb_width, q_off, rows=128, unroll=8):
    n = pg.shape[0]
    heads = sb_width // HEAD_DIM
    c0 = q_off // HEAD_DIM
    assert seq % (rows * unroll) == 0 and seq >= 2 * rows

    def spec(section):
        return pl.BlockSpec((seq, HEAD_DIM), lambda b, h: (b, c0 + section * heads + h))

    return pl.pallas_call(
        functools.partial(_attn_kernel, rows=rows, unroll=unroll),
        grid=(batch, heads),
        in_specs=[spec(0), spec(1), spec(2), spec(3)],
        out_specs=pl.BlockSpec((seq, HEAD_DIM), lambda b, h: (b, h)),
        out_shape=jax.ShapeDtypeStruct((n, sb_width), BF16),
        compiler_params=_params("parallel", "parallel"),
        name="stickbreak_attn",
    )(pg, pg, pg, pg)


def _mix_kernel(xz_ref, ub_ref, cz_ref, xa_h_ref, u_h_ref, cg_h_ref, yb_ref,
                pw_ref, ps_ref, cw_ref, wa_ref, wb_ref, wc_ref, g0_ref, g1_ref, g2_ref,
                o_ref, ya_ref, yc_ref, *, per_seq):
    tm = xz_ref.shape[0]
    pool_width = xz_ref.shape[1] // 2
    conv_ch = ub_ref.shape[1] // 2
    group = pool_width // len(POOL_WINDOWS)
    i = pl.program_id(0)

    b = jnp.dot(yb_ref[...], wb_ref[...], preferred_element_type=F32)

    seq_start = (i % per_seq) == 0
    keep = jnp.where(seq_start, 0.0, 1.0)
    pos = (i % per_seq) * tm + lax.broadcasted_iota(jnp.int32, (tm, 1), 0)

    for g, w in enumerate(POOL_WINDOWS):
        cs = slice(g * group, (g + 1) * group)
        xa = xz_ref[:, cs].astype(F32)
        halo = xa_h_ref[:, cs].astype(F32) * keep
        s = jnp.concatenate([halo, xa], axis=0)
        shift = 1
        while shift < w:
            s = s + pltpu.roll(s, shift, axis=0)
            shift *= 2
        count = jnp.minimum(pos + 1, w).astype(F32)
        mixed = s[HALO:, :] / count - xa
        y = jnp.dot(mixed.astype(BF16), pw_ref[g], preferred_element_type=F32)
        za = xz_ref[:, pool_width + g * group: pool_width + (g + 1) * group].astype(F32)
        ya_ref[:, cs] = (y * ps_ref[:, cs] * _silu(za)).astype(BF16)

    u = ub_ref[:, :conv_ch].astype(F32)
    bg = ub_ref[:, conv_ch:].astype(F32)
    cg = cz_ref[:, :conv_ch].astype(F32)
    zc = cz_ref[:, conv_ch:].astype(F32)
    v = cg * u
    vh = cg_h_ref[...].astype(F32) * u_h_ref[...].astype(F32) * keep
    ve = jnp.concatenate([vh, v], axis=0)
    y = cw_ref[CONV_WIDTH - 1: CONV_WIDTH, :] * v
    for tap in range(CONV_WIDTH - 1):
        back = CONV_WIDTH - 1 - tap
        y = y + cw_ref[tap: tap + 1, :] * pltpu.roll(ve, back, axis=0)[HALO:, :]
    yc_ref[...] = (bg * y * _silu(zc)).astype(BF16)

    a = jnp.dot(ya_ref[...], wa_ref[...], preferred_element_type=F32)
    c = jnp.dot(yc_ref[...], wc_ref[...], preferred_element_type=F32)
    merged = (g0_ref[...].astype(F32) * a + g1_ref[...].astype(F32) * b
              + g2_ref[...].astype(F32) * c)
    o_ref[...] = merged.astype(o_ref.dtype)


def _mix(pg, yb, pool_w, pool_scale, conv_w, w_a, w_b, w_c, seq, tm):
    n = pg.shape[0]
    pool_width, d = w_a.shape
    sb_width = w_b.shape[0]
    conv_ch = w_c.shape[0]
    per_seq = seq // tm
    u_off = 2 * pool_width + 4 * sb_width
    gate_off = u_off + 4 * conv_ch
    wide = 2 * pool_width
    assert 2 * conv_ch == wide and u_off % wide == 0 and gate_off % d == 0

    def halo_rows(i):
        return jnp.maximum(i * (tm // HALO) - 1, 0)

    def resident(shape):
        return pl.BlockSpec(shape, lambda i: (0,) * len(shape), pipeline_mode=pl.Buffered(1))

    in_specs = [
        pl.BlockSpec((tm, wide), lambda i: (i, 0)),
        pl.BlockSpec((tm, wide), lambda i: (i, u_off // wide)),
        pl.BlockSpec((tm, wide), lambda i: (i, u_off // wide + 1)),
        pl.BlockSpec((HALO, pool_width), lambda i: (halo_rows(i), 0)),
        pl.BlockSpec((HALO, conv_ch), lambda i: (halo_rows(i), u_off // conv_ch)),
        pl.BlockSpec((HALO, conv_ch), lambda i: (halo_rows(i), u_off // conv_ch + 2)),
        pl.BlockSpec((tm, sb_width), lambda i: (i, 0)),
        resident(pool_w.shape),
        resident((1, pool_width)),
        resident(conv_w.shape),
        resident(w_a.shape),
        resident(w_b.shape),
        resident(w_c.shape),
        pl.BlockSpec((tm, d), lambda i: (i, gate_off // d)),
        pl.BlockSpec((tm, d), lambda i: (i, gate_off // d + 1)),
        pl.BlockSpec((tm, d), lambda i: (i, gate_off // d + 2)),
    ]
    return pl.pallas_call(
        functools.partial(_mix_kernel, per_seq=per_seq),
        grid=(n // tm,),
        in_specs=in_specs,
        out_specs=pl.BlockSpec((tm, d), lambda i: (i, 0)),
        out_shape=jax.ShapeDtypeStruct((n, d), BF16),
        scratch_shapes=[pltpu.VMEM((tm, pool_width), BF16), pltpu.VMEM((tm, conv_ch), BF16)],
        compiler_params=_params("parallel"),
        name="mix_merge",
    )(pg, pg, pg, pg, pg, pg, yb, pool_w, pool_scale, conv_w, w_a, w_b, w_c, pg, pg, pg)


def _out_kernel(m_ref, w_ref, x_ref, rg_ref, *rest, final):
    if final:
        fg_ref, o_ref = rest
    else:
        (o_ref,) = rest
    y = jnp.dot(m_ref[...], w_ref[...], preferred_element_type=F32)
    xn = x_ref[...] + rg_ref[0] * y
    if final:
        ms = jnp.mean(xn * xn, axis=-1, keepdims=True)
        xn = xn * lax.rsqrt(ms + RMS_EPS) * fg_ref[...]
    o_ref[...] = xn


def _out(merged, w_out, x2, res_gate3, final_g, seq, tm):
    n, d = x2.shape
    per_seq = seq // tm
    final = final_g is not None
    in_specs = [
        pl.BlockSpec((tm, d), lambda i: (i, 0)),
        pl.BlockSpec((d, d), lambda i: (0, 0)),
        pl.BlockSpec((tm, d), lambda i: (i, 0)),
        pl.BlockSpec((1, 1, d), lambda i: (i // per_seq, 0, 0)),
    ]
    args = [merged, w_out, x2, res_gate3]
    if final:
        in_specs.append(pl.BlockSpec((1, d), lambda i: (0, 0)))
        args.append(final_g)
    return pl.pallas_call(
        functools.partial(_out_kernel, final=final),
        grid=(n // tm,),
        in_specs=in_specs,
        out_specs=pl.BlockSpec((tm, d), lambda i: (i, 0)),
        out_shape=jax.ShapeDtypeStruct((n, d), F32),
        compiler_params=_params("parallel"),
        name="out_proj_final" if final else "out_proj",
    )(*args)


def kernel(x, c, norm_g, w_ada, b_ada, w_in, pool_w, pool_scale, conv_w, w_br_a, w_br_b,
           w_br_c, w_gate, b_gate, w_out, final_g):
    batch, seq, d = x.shape
    depth = w_in.shape[0]
    pool_width = w_br_a.shape[1]
    sb_width = w_br_b.shape[1]
    q_off = 2 * pool_width
    n = batch * seq
    tm = min(512, seq)

    c_pad = jnp.pad(c, ((0, -batch % 8), (0, 0)))
    mod = _ada(c_pad, w_ada, b_ada)[:, :batch]
    x2 = x.reshape(n, d)

    for l in range(depth):
        shift3 = mod[l, :, :d].reshape(batch, 1, d)
        scale3 = mod[l, :, d:2 * d].reshape(batch, 1, d)
        res_gate3 = mod[l, :, 2 * d:].reshape(batch, 1, d)
        g = norm_g[l].reshape(1, d)

        w_q_scaled = w_in[l].at[:, q_off:q_off + sb_width].multiply(HEAD_DIM ** -0.5)
        w_all = jnp.concatenate([w_q_scaled, w_gate[l]], axis=1).astype(BF16)
        pg = _inproj(x2, g, scale3, shift3, w_all, b_gate[l].reshape(1, -1), seq, tm, 2048)
        yb = _attention(pg, batch, seq, sb_width, q_off)
        merged = _mix(pg, yb, pool_w[l].astype(BF16), pool_scale[l].reshape(1, -1),
                      conv_w[l], w_br_a[l].astype(BF16), w_br_b[l].astype(BF16),
                      w_br_c[l].astype(BF16), seq, min(256, seq))
        x2 = _out(merged, w_out[l].astype(BF16), x2, res_gate3,
                  final_g.reshape(1, d) if l == depth - 1 else None, seq, tm)

    return x2.reshape(batch, seq, d)
```

```python
import functools

import jax
import jax.numpy as jnp
from jax import lax
from jax.experimental import pallas as pl
from jax.experimental.pallas import tpu as pltpu

RMS_EPS = 1e-6
HEAD_DIM = 128
POOL_WINDOWS = (2, 4, 8, 16)
CONV_WIDTH = 3
HALO = 16
VMEM_LIMIT_BYTES = 56 * 1024 * 1024

F32 = jnp.float32
BF16 = jnp.bfloat16


def _params(*sem):
    return pltpu.CompilerParams(dimension_semantics=sem, vmem_limit_bytes=VMEM_LIMIT_BYTES)


def _silu(v):
    return v * jax.nn.sigmoid(v)


def _ada_kernel(c_ref, w_ref, b_ref, o_ref):
    s = _silu(c_ref[...])
    o_ref[0] = jnp.dot(s, w_ref[0], precision=lax.Precision.HIGHEST,
                       preferred_element_type=F32) + b_ref[0]


def _ada(c_pad, w_ada, b_ada, tn=768):
    depth, d, n3 = w_ada.shape
    rows = c_pad.shape[0]
    return pl.pallas_call(
        _ada_kernel,
        grid=(depth, n3 // tn),
        in_specs=[
            pl.BlockSpec((rows, d), lambda l, j: (0, 0)),
            pl.BlockSpec((1, d, tn), lambda l, j: (l, 0, j)),
            pl.BlockSpec((1, 1, tn), lambda l, j: (l, 0, j)),
        ],
        out_specs=pl.BlockSpec((1, rows, tn), lambda l, j: (l, 0, j)),
        out_shape=jax.ShapeDtypeStruct((depth, rows, n3), F32),
        compiler_params=_params("parallel", "parallel"),
        name="ada_mod",
    )(c_pad, w_ada, b_ada.reshape(depth, 1, n3))


def _inproj_kernel(x_ref, g_ref, sc_ref, sh_ref, w_ref, b_ref, o_ref, h_ref, *, n_plain, chunk):
    j = pl.program_id(1)

    @pl.when(j == 0)
    def _():
        g = g_ref[...]
        sc = 1.0 + sc_ref[0]
        sh = sh_ref[0]

        def body(r, carry):
            rows = pl.ds(pl.multiple_of(r * chunk, chunk), chunk)
            xc = x_ref[rows, :]
            ms = jnp.mean(xc * xc, axis=-1, keepdims=True)
            y = xc * lax.rsqrt(ms + RMS_EPS) * g
            h_ref[rows, :] = (y * sc + sh).astype(BF16)
            return carry

        lax.fori_loop(0, x_ref.shape[0] // chunk, body, 0)

    @pl.when(j < n_plain)
    def _():
        acc = jnp.dot(h_ref[...], w_ref[...], preferred_element_type=F32)
        o_ref[...] = acc.astype(o_ref.dtype)

    @pl.when(j >= n_plain)
    def _():
        acc = jnp.dot(h_ref[...], w_ref[...], preferred_element_type=F32)
        o_ref[...] = jax.nn.sigmoid(acc + b_ref[...]).astype(o_ref.dtype)


def _inproj(x2, g, scale3, shift3, w_all, b_gate, seq, tm, tn):
    n, d = x2.shape
    ncol = w_all.shape[1]
    n_plain = (ncol - b_gate.shape[1]) // tn
    per_seq = seq // tm
    return pl.pallas_call(
        functools.partial(_inproj_kernel, n_plain=n_plain, chunk=64),
        grid=(n // tm, ncol // tn),
        in_specs=[
            pl.BlockSpec((tm, d), lambda i, j: (i, 0)),
            pl.BlockSpec((1, d), lambda i, j: (0, 0)),
            pl.BlockSpec((1, 1, d), lambda i, j: (i // per_seq, 0, 0)),
            pl.BlockSpec((1, 1, d), lambda i, j: (i // per_seq, 0, 0)),
            pl.BlockSpec((d, tn), lambda i, j: (0, j)),
            pl.BlockSpec((1, tn), lambda i, j: (0, jnp.maximum(j - n_plain, 0))),
        ],
        out_specs=pl.BlockSpec((tm, tn), lambda i, j: (i, j)),
        out_shape=jax.ShapeDtypeStruct((n, ncol), BF16),
        scratch_shapes=[pltpu.VMEM((tm, d), BF16)],
        compiler_params=_params("parallel", "arbitrary"),
        name="in_proj",
    )(x2, g, scale3, shift3, w_all, b_gate)


LOG2E = 1.4426950408889634
DEAD_CARRY = 88.0
MASKED_LOGIT = -1e30


def _attn_kernel(q_ref, k_ref, v_ref, zb_ref, o_ref, *, rows, unroll, wave):
    seq = q_ref.shape[0]
    win = 2 * rows
    col = lax.broadcasted_iota(jnp.int32, (rows, win), 1)
    ahead = col - lax.broadcasted_iota(jnp.int32, (rows, win), 0)
    suffix = (lax.broadcasted_iota(jnp.int32, (win, win), 0)
              > lax.broadcasted_iota(jnp.int32, (win, win), 1)).astype(BF16)

    def chunk_start(top):
        return pl.multiple_of(jnp.maximum(top - win, 0), rows)

    def scores(q, start, visible):
        z = lax.dot_general(q, k_ref[pl.ds(start, win), :], (((1,), (1,)), ((), ())),
                            preferred_element_type=F32)
        z = jnp.where(visible, z, MASKED_LOGIT)
        return z, jnp.maximum(z, 0.0) + jnp.log(1.0 + jnp.exp2(jnp.abs(z) * -LOG2E))

    def later_sums(sp):
        return jnp.dot(sp.astype(BF16), suffix, preferred_element_type=F32)

    def weighted(z, sp, between, start):
        a = jnp.exp2((z - sp - between) * LOG2E)
        return jnp.dot(a.astype(BF16), v_ref[pl.ds(start, win), :], preferred_element_type=F32)

    def row_sums(sp, between):
        return between[:, 0:1] + sp[:, 0:1]

    def group_rows(g):
        return pl.ds(pl.multiple_of(g * rows, rows), rows)

    def step(it, _):
        groups = [it * unroll + u for u in range(unroll)]
        qs = [q_ref[group_rows(g), :] for g in groups]
        starts1 = [chunk_start((g + 1) * rows) for g in groups]
        starts2 = [chunk_start(s1) for s1 in starts1]
        starts = starts1 + starts2
        visible = ([ahead < g * rows - s1 for g, s1 in zip(groups, starts1)]
                   + [col < s1 - s2 for s1, s2 in zip(starts1, starts2)])
        n_chunks = 2 * unroll
        zs, betweens, pvs = [None] * n_chunks, [None] * n_chunks, [None] * n_chunks
        for w in range(n_chunks // wave + 2):
            for c in range(w * wave, min((w + 1) * wave, n_chunks)):
                zs[c] = scores(qs[c % unroll], starts[c], visible[c])
            for c in range(max(w - 1, 0) * wave, min(w * wave, n_chunks)):
                betweens[c] = later_sums(zs[c][1])
            for c in range(max(w - 2, 0) * wave, min((w - 1) * wave, n_chunks)):
                pvs[c] = weighted(zs[c][0], zs[c][1], betweens[c], starts[c])
        sums = [row_sums(sp, b) for (_, sp), b in zip(zs, betweens)]

        accs = [pvs[u] + jnp.exp(-sums[u]) * pvs[unroll + u] for u in range(unroll)]
        carries = [sums[u] + sums[unroll + u] for u in range(unroll)]
        alive = [jnp.min(carry) < DEAD_CARRY for carry in carries]

        for u, g in enumerate(groups):
            def sweep_earlier(q=qs[u], acc=accs[u], carry=carries[u], top=starts2[u]):
                def live(state):
                    top, carry, _ = state
                    return jnp.logical_and(top > 0, jnp.min(carry) < DEAD_CARRY)

                def earlier_chunk(state):
                    top, carry, acc = state
                    start = chunk_start(top)
                    z, sp = scores(q, start, col < top - start)
                    between = later_sums(sp)
                    pv = weighted(z, sp, between, start)
                    return start, carry + row_sums(sp, between), acc + jnp.exp(-carry) * pv

                return lax.while_loop(live, earlier_chunk, (top, carry, acc))[2]

            acc = lax.cond(alive[u], sweep_earlier, lambda acc=accs[u]: acc)
            zb = zb_ref[group_rows(g), :].astype(F32)
            o_ref[group_rows(g), :] = (acc * _silu(zb)).astype(o_ref.dtype)
        return 0

    lax.fori_loop(0, seq // (rows * unroll), step, 0)


def _attention(pg, batch, seq, sb_width, q_off, rows=128, unroll=8, wave=4):
    n = pg.shape[0]
    heads = sb_width // HEAD_DIM
    c0 = q_off // HEAD_DIM
    assert seq % (rows * unroll) == 0 and seq >= 2 * rows

    def spec(section):
        return pl.BlockSpec((seq, HEAD_DIM), lambda b, h: (b, c0 + section * heads + h))

    return pl.pallas_call(
        functools.partial(_attn_kernel, rows=rows, unroll=unroll, wave=wave),
        grid=(batch, heads),
        in_specs=[spec(0), spec(1), spec(2), spec(3)],
        out_specs=pl.BlockSpec((seq, HEAD_DIM), lambda b, h: (b, h)),
        out_shape=jax.ShapeDtypeStruct((n, sb_width), BF16),
        compiler_params=_params("parallel", "parallel"),
        name="stickbreak_attn",
    )(pg, pg, pg, pg)


def _mix_kernel(xz_ref, ub_ref, cz_ref, xa_h_ref, u_h_ref, cg_h_ref, yb_ref,
                pw_ref, ps_ref, cw_ref, wa_ref, wb_ref, wc_ref, g0_ref, g1_ref, g2_ref,
                o_ref, ya_ref, yc_ref, *, per_seq):
    tm = xz_ref.shape[0]
    pool_width = xz_ref.shape[1] // 2
    conv_ch = ub_ref.shape[1] // 2
    group = pool_width // len(POOL_WINDOWS)
    i = pl.program_id(0)

    b = jnp.dot(yb_ref[...], wb_ref[...], preferred_element_type=F32)

    seq_start = (i % per_seq) == 0
    keep = jnp.where(seq_start, 0.0, 1.0)
    pos = (i % per_seq) * tm + lax.broadcasted_iota(jnp.int32, (tm, 1), 0)

    for g, w in enumerate(POOL_WINDOWS):
        cs = slice(g * group, (g + 1) * group)
        xa = xz_ref[:, cs].astype(F32)
        halo = xa_h_ref[:, cs].astype(F32) * keep
        s = jnp.concatenate([halo, xa], axis=0)
        shift = 1
        while shift < w:
            s = s + pltpu.roll(s, shift, axis=0)
            shift *= 2
        count = jnp.minimum(pos + 1, w).astype(F32)
        mixed = s[HALO:, :] / count - xa
        y = jnp.dot(mixed.astype(BF16), pw_ref[g], preferred_element_type=F32)
        za = xz_ref[:, pool_width + g * group: pool_width + (g + 1) * group].astype(F32)
        ya_ref[:, cs] = (y * ps_ref[:, cs] * _silu(za)).astype(BF16)

    u = ub_ref[:, :conv_ch].astype(F32)
    bg = ub_ref[:, conv_ch:].astype(F32)
    cg = cz_ref[:, :conv_ch].astype(F32)
    zc = cz_ref[:, conv_ch:].astype(F32)
    v = cg * u
    vh = cg_h_ref[...].astype(F32) * u_h_ref[...].astype(F32) * keep
    ve = jnp.concatenate([vh, v], axis=0)
    y = cw_ref[CONV_WIDTH - 1: CONV_WIDTH, :] * v
    for tap in range(CONV_WIDTH - 1):
        back = CONV_WIDTH - 1 - tap
        y = y + cw_ref[tap: tap + 1, :] * pltpu.roll(ve, back, axis=0)[HALO:, :]
    yc_ref[...] = (bg * y * _silu(zc)).astype(BF16)

    a = jnp.dot(ya_ref[...], wa_ref[...], preferred_element_type=F32)
    c = jnp.dot(yc_ref[...], wc_ref[...], preferred_element_type=F32)
    merged = (g0_ref[...].astype(F32) * a + g1_ref[...].astype(F32) * b
              + g2_ref[...].astype(F32) * c)
    o_ref[...] = merged.astype(o_ref.dtype)


def _mix(pg, yb, pool_w, pool_scale, conv_w, w_a, w_b, w_c, seq, tm):
    n = pg.shape[0]
    pool_width, d = w_a.shape
    sb_width = w_b.shape[0]
    conv_ch = w_c.shape[0]
    per_seq = seq // tm
    u_off = 2 * pool_width + 4 * sb_width
    gate_off = u_off + 4 * conv_ch
    wide = 2 * pool_width
    assert 2 * conv_ch == wide and u_off % wide == 0 and gate_off % d == 0

    def halo_rows(i):
        return jnp.maximum(i * (tm // HALO) - 1, 0)

    def resident(shape):
        return pl.BlockSpec(shape, lambda i: (0,) * len(shape), pipeline_mode=pl.Buffered(1))

    in_specs = [
        pl.BlockSpec((tm, wide), lambda i: (i, 0)),
        pl.BlockSpec((tm, wide), lambda i: (i, u_off // wide)),
        pl.BlockSpec((tm, wide), lambda i: (i, u_off // wide + 1)),
        pl.BlockSpec((HALO, pool_width), lambda i: (halo_rows(i), 0)),
        pl.BlockSpec((HALO, conv_ch), lambda i: (halo_rows(i), u_off // conv_ch)),
        pl.BlockSpec((HALO, conv_ch), lambda i: (halo_rows(i), u_off // conv_ch + 2)),
        pl.BlockSpec((tm, sb_width), lambda i: (i, 0)),
        resident(pool_w.shape),
        resident((1, pool_width)),
        resident(conv_w.shape),
        resident(w_a.shape),
        resident(w_b.shape),
        resident(w_c.shape),
        pl.BlockSpec((tm, d), lambda i: (i, gate_off // d)),
        pl.BlockSpec((tm, d), lambda i: (i, gate_off // d + 1)),
        pl.BlockSpec((tm, d), lambda i: (i, gate_off // d + 2)),
    ]
    return pl.pallas_call(
        functools.partial(_mix_kernel, per_seq=per_seq),
        grid=(n // tm,),
        in_specs=in_specs,
        out_specs=pl.BlockSpec((tm, d), lambda i: (i, 0)),
        out_shape=jax.ShapeDtypeStruct((n, d), BF16),
        scratch_shapes=[pltpu.VMEM((tm, pool_width), BF16), pltpu.VMEM((tm, conv_ch), BF16)],
        compiler_params=_params("parallel"),
        name="mix_merge",
    )(pg, pg, pg, pg, pg, pg, yb, pool_w, pool_scale, conv_w, w_a, w_b, w_c, pg, pg, pg)


def _out_kernel(m_ref, w_ref, x_ref, rg_ref, *rest, final):
    if final:
        fg_ref, o_ref = rest
    else:
        (o_ref,) = rest
    y = jnp.dot(m_ref[...], w_ref[...], preferred_element_type=F32)
    xn = x_ref[...] + rg_ref[0] * y
    if final:
        ms = jnp.mean(xn * xn, axis=-1, keepdims=True)
        xn = xn * lax.rsqrt(ms + RMS_EPS) * fg_ref[...]
    o_ref[...] = xn


def _out(merged, w_out, x2, res_gate3, final_g, seq, tm):
    n, d = x2.shape
    per_seq = seq // tm
    final = final_g is not None
    in_specs = [
        pl.BlockSpec((tm, d), lambda i: (i, 0)),
        pl.BlockSpec((d, d), lambda i: (0, 0)),
        pl.BlockSpec((tm, d), lambda i: (i, 0)),
        pl.BlockSpec((1, 1, d), lambda i: (i // per_seq, 0, 0)),
    ]
    args = [merged, w_out, x2, res_gate3]
    if final:
        in_specs.append(pl.BlockSpec((1, d), lambda i: (0, 0)))
        args.append(final_g)
    return pl.pallas_call(
        functools.partial(_out_kernel, final=final),
        grid=(n // tm,),
        in_specs=in_specs,
        out_specs=pl.BlockSpec((tm, d), lambda i: (i, 0)),
        out_shape=jax.ShapeDtypeStruct((n, d), F32),
        compiler_params=_params("parallel"),
        name="out_proj_final" if final else "out_proj",
    )(*args)


def kernel(x, c, norm_g, w_ada, b_ada, w_in, pool_w, pool_scale, conv_w, w_br_a, w_br_b,
           w_br_c, w_gate, b_gate, w_out, final_g):
    batch, seq, d = x.shape
    depth = w_in.shape[0]
    pool_width = w_br_a.shape[1]
    sb_width = w_br_b.shape[1]
    q_off = 2 * pool_width
    n = batch * seq
    tm = min(512, seq)

    c_pad = jnp.pad(c, ((0, -batch % 8), (0, 0)))
    mod = _ada(c_pad, w_ada, b_ada)[:, :batch]
    x2 = x.reshape(n, d)

    for l in range(depth):
        shift3 = mod[l, :, :d].reshape(batch, 1, d)
        scale3 = mod[l, :, d:2 * d].reshape(batch, 1, d)
        res_gate3 = mod[l, :, 2 * d:].reshape(batch, 1, d)
        g = norm_g[l].reshape(1, d)

        w_q_scaled = w_in[l].at[:, q_off:q_off + sb_width].multiply(HEAD_DIM ** -0.5)
        w_all = jnp.concatenate([w_q_scaled, w_gate[l]], axis=1).astype(BF16)
        pg = _inproj(x2, g, scale3, shift3, w_all, b_gate[l].reshape(1, -1), seq, tm, 2048)
        yb = _attention(pg, batch, seq, sb_width, q_off)
        merged = _mix(pg, yb, pool_w[l].astype(BF16), pool_scale[l].reshape(1, -1),
                      conv_w[l], w_br_a[l].astype(BF16), w_br_b[l].astype(BF16),
                      w_br_c[l].astype(BF16), seq, min(256, seq))
        x2 = _out(merged, w_out[l].astype(BF16), x2, res_gate3,
                  final_g.reshape(1, d) if l == depth - 1 else None, seq, tm)

    return x2.reshape(batch, seq, d)
```

```python
import functools

import jax
import jax.numpy as jnp
from jax import lax
from jax.experimental import pallas as pl
from jax.experimental.pallas import tpu as pltpu

RMS_EPS = 1e-6
HEAD_DIM = 128
POOL_WINDOWS = (2, 4, 8, 16)
CONV_WIDTH = 3
HALO = 16
VMEM_LIMIT_BYTES = 56 * 1024 * 1024
PROJ_TN = 2048

F32 = jnp.float32
BF16 = jnp.bfloat16


def _params(*sem):
    return pltpu.CompilerParams(dimension_semantics=sem, vmem_limit_bytes=VMEM_LIMIT_BYTES)


def _silu(v):
    return v * jax.nn.sigmoid(v)


def _ada_kernel(c_ref, w_ref, b_ref, o_ref):
    s = _silu(c_ref[...])
    o_ref[0] = jnp.dot(s, w_ref[0], precision=lax.Precision.HIGHEST,
                       preferred_element_type=F32) + b_ref[0]


def _ada(c_pad, w_ada, b_ada, tn=768):
    depth, d, n3 = w_ada.shape
    rows = c_pad.shape[0]
    return pl.pallas_call(
        _ada_kernel,
        grid=(depth, n3 // tn),
        in_specs=[
            pl.BlockSpec((rows, d), lambda l, j: (0, 0)),
            pl.BlockSpec((1, d, tn), lambda l, j: (l, 0, j)),
            pl.BlockSpec((1, 1, tn), lambda l, j: (l, 0, j)),
        ],
        out_specs=pl.BlockSpec((1, rows, tn), lambda l, j: (l, 0, j)),
        out_shape=jax.ShapeDtypeStruct((depth, rows, n3), F32),
        compiler_params=_params("parallel", "parallel"),
        name="ada_mod",
    )(c_pad, w_ada, b_ada.reshape(depth, 1, n3))


CAST_ROWS = 64


def _modulated_norm(x, g_ref, sc_ref, sh_ref):
    ms = jnp.mean(x * x, axis=-1, keepdims=True)
    y = x * lax.rsqrt(ms + RMS_EPS) * g_ref[...]
    return (y * (1.0 + sc_ref[0]) + sh_ref[0]).astype(BF16)


def _norm_mod_kernel(x_ref, g_ref, sc_ref, sh_ref, h_ref):
    h_ref[...] = _modulated_norm(x_ref[...], g_ref, sc_ref, sh_ref)


def _norm_mod(x2, g, scale3, shift3, seq, tm):
    n, d = x2.shape
    per_seq = seq // tm
    return pl.pallas_call(
        _norm_mod_kernel,
        grid=(n // tm,),
        in_specs=[
            pl.BlockSpec((tm, d), lambda i: (i, 0)),
            pl.BlockSpec((1, d), lambda i: (0, 0)),
            pl.BlockSpec((1, 1, d), lambda i: (i // per_seq, 0, 0)),
            pl.BlockSpec((1, 1, d), lambda i: (i // per_seq, 0, 0)),
        ],
        out_specs=pl.BlockSpec((tm, d), lambda i: (i, 0)),
        out_shape=jax.ShapeDtypeStruct((n, d), BF16),
        compiler_params=_params("parallel"),
        name="norm_mod",
    )(x2, g, scale3, shift3)


def _proj_kernel(h_ref, w_ref, cs_ref, *rest, gated):
    if gated:
        b_ref, o_ref, wb_ref = rest
    else:
        o_ref, wb_ref = rest

    @pl.when(pl.program_id(1) == 0)
    def _():
        cs = cs_ref[...]

        def body(r, carry):
            rows = pl.ds(pl.multiple_of(r * CAST_ROWS, CAST_ROWS), CAST_ROWS)
            wb_ref[rows, :] = (w_ref[rows, :] * cs).astype(BF16)
            return carry

        lax.fori_loop(0, w_ref.shape[0] // CAST_ROWS, body, 0)

    acc = jnp.dot(h_ref[...], wb_ref[...], preferred_element_type=F32)
    if gated:
        acc = jax.nn.sigmoid(acc + b_ref[...])
    o_ref[...] = acc.astype(o_ref.dtype)


def _proj(h, w, col_scale, bias, tm, tn):
    n, d = h.shape
    ncol = w.shape[1]
    gated = bias is not None
    in_specs = [
        pl.BlockSpec((tm, d), lambda j, i: (i, 0)),
        pl.BlockSpec((d, tn), lambda j, i: (0, j)),
        pl.BlockSpec((1, tn), lambda j, i: (0, j)),
    ]
    args = [h, w, col_scale]
    if gated:
        in_specs.append(pl.BlockSpec((1, tn), lambda j, i: (0, j)))
        args.append(bias)
    return pl.pallas_call(
        functools.partial(_proj_kernel, gated=gated),
        grid=(ncol // tn, n // tm),
        in_specs=in_specs,
        out_specs=pl.BlockSpec((tm, tn), lambda j, i: (i, j)),
        out_shape=jax.ShapeDtypeStruct((n, ncol), BF16),
        scratch_shapes=[pltpu.VMEM((d, tn), BF16)],
        compiler_params=_params("arbitrary", "arbitrary"),
        name="gate_proj" if gated else "in_proj",
    )(*args)


LOG2E = 1.4426950408889634
DEAD_CARRY = 88.0
MASKED_LOGIT = -1e30


def _attn_kernel(q_ref, k_ref, v_ref, zb_ref, o_ref, *, rows, unroll, wave):
    seq = q_ref.shape[0]
    win = 2 * rows
    col = lax.broadcasted_iota(jnp.int32, (rows, win), 1)
    ahead = col - lax.broadcasted_iota(jnp.int32, (rows, win), 0)
    suffix = (lax.broadcasted_iota(jnp.int32, (win, win), 0)
              > lax.broadcasted_iota(jnp.int32, (win, win), 1)).astype(BF16)

    def chunk_start(top):
        return pl.multiple_of(jnp.maximum(top - win, 0), rows)

    def scores(q, start, visible):
        z = lax.dot_general(q, k_ref[pl.ds(start, win), :], (((1,), (1,)), ((), ())),
                            preferred_element_type=F32)
        z = jnp.where(visible, z, MASKED_LOGIT)
        return z, jnp.maximum(z, 0.0) + jnp.log(1.0 + jnp.exp2(jnp.abs(z) * -LOG2E))

    def later_sums(sp):
        return jnp.dot(sp.astype(BF16), suffix, preferred_element_type=F32)

    def weighted(z, sp, between, start):
        a = jnp.exp2((z - sp - between) * LOG2E)
        return jnp.dot(a.astype(BF16), v_ref[pl.ds(start, win), :], preferred_element_type=F32)

    def row_sums(sp, between):
        return between[:, 0:1] + sp[:, 0:1]

    def group_rows(g):
        return pl.ds(pl.multiple_of(g * rows, rows), rows)

    def step(it, _):
        groups = [it * unroll + u for u in range(unroll)]
        qs = [q_ref[group_rows(g), :] for g in groups]
        starts1 = [chunk_start((g + 1) * rows) for g in groups]
        starts2 = [chunk_start(s1) for s1 in starts1]
        starts = starts1 + starts2
        visible = ([ahead < g * rows - s1 for g, s1 in zip(groups, starts1)]
                   + [col < s1 - s2 for s1, s2 in zip(starts1, starts2)])
        n_chunks = 2 * unroll
        zs, betweens, pvs = [None] * n_chunks, [None] * n_chunks, [None] * n_chunks
        for w in range(n_chunks // wave + 2):
            for c in range(w * wave, min((w + 1) * wave, n_chunks)):
                zs[c] = scores(qs[c % unroll], starts[c], visible[c])
            for c in range(max(w - 1, 0) * wave, min(w * wave, n_chunks)):
                betweens[c] = later_sums(zs[c][1])
            for c in range(max(w - 2, 0) * wave, min((w - 1) * wave, n_chunks)):
                pvs[c] = weighted(zs[c][0], zs[c][1], betweens[c], starts[c])
        sums = [row_sums(sp, b) for (_, sp), b in zip(zs, betweens)]

        accs = [pvs[u] + jnp.exp(-sums[u]) * pvs[unroll + u] for u in range(unroll)]
        carries = [sums[u] + sums[unroll + u] for u in range(unroll)]
        alive = [jnp.min(carry) < DEAD_CARRY for carry in carries]

        for u, g in enumerate(groups):
            def sweep_earlier(q=qs[u], acc=accs[u], carry=carries[u], top=starts2[u]):
                def live(state):
                    top, carry, _ = state
                    return jnp.logical_and(top > 0, jnp.min(carry) < DEAD_CARRY)

                def earlier_chunk(state):
                    top, carry, acc = state
                    start = chunk_start(top)
                    z, sp = scores(q, start, col < top - start)
                    between = later_sums(sp)
                    pv = weighted(z, sp, between, start)
                    return start, carry + row_sums(sp, between), acc + jnp.exp(-carry) * pv

                return lax.while_loop(live, earlier_chunk, (top, carry, acc))[2]

            acc = lax.cond(alive[u], sweep_earlier, lambda acc=accs[u]: acc)
            zb = zb_ref[group_rows(g), :].astype(F32)
            o_ref[group_rows(g), :] = (acc * _silu(zb)).astype(o_ref.dtype)
        return 0

    lax.fori_loop(0, seq // (rows * unroll), step, 0)


def _attention(pg, batch, seq, sb_width, q_off, rows=128, unroll=8, wave=4):
    n = pg.shape[0]
    heads = sb_width // HEAD_DIM
    c0 = q_off // HEAD_DIM
    assert seq % (rows * unroll) == 0 and seq >= 2 * rows

    def spec(section):
        return pl.BlockSpec((seq, HEAD_DIM), lambda b, h: (b, c0 + section * heads + h))

    return pl.pallas_call(
        functools.partial(_attn_kernel, rows=rows, unroll=unroll, wave=wave),
        grid=(batch, heads),
        in_specs=[spec(0), spec(1), spec(2), spec(3)],
        out_specs=pl.BlockSpec((seq, HEAD_DIM), lambda b, h: (b, h)),
        out_shape=jax.ShapeDtypeStruct((n, sb_width), BF16),
        compiler_params=_params("parallel", "parallel"),
        name="stickbreak_attn",
    )(pg, pg, pg, pg)


def _mix_kernel(xz_ref, ub_ref, cz_ref, xa_h_ref, u_h_ref, cg_h_ref, yb_ref,
                pw_ref, ps_ref, cw_ref, wa_ref, wb_ref, wc_ref, g0_ref, g1_ref, g2_ref,
                o_ref, ya_ref, yc_ref, *, per_seq):
    tm = xz_ref.shape[0]
    pool_width = xz_ref.shape[1] // 2
    conv_ch = ub_ref.shape[1] // 2
    group = pool_width // len(POOL_WINDOWS)
    i = pl.program_id(0)

    b = jnp.dot(yb_ref[...], wb_ref[...], preferred_element_type=F32)

    seq_start = (i % per_seq) == 0
    keep = jnp.where(seq_start, 0.0, 1.0)
    pos = (i % per_seq) * tm + lax.broadcasted_iota(jnp.int32, (tm, 1), 0)

    for g, w in enumerate(POOL_WINDOWS):
        cs = slice(g * group, (g + 1) * group)
        xa = xz_ref[:, cs].astype(F32)
        halo = xa_h_ref[:, cs].astype(F32) * keep
        s = jnp.concatenate([halo, xa], axis=0)
        shift = 1
        while shift < w:
            s = s + pltpu.roll(s, shift, axis=0)
            shift *= 2
        count = jnp.minimum(pos + 1, w).astype(F32)
        mixed = s[HALO:, :] / count - xa
        y = jnp.dot(mixed.astype(BF16), pw_ref[g], preferred_element_type=F32)
        za = xz_ref[:, pool_width + g * group: pool_width + (g + 1) * group].astype(F32)
        ya_ref[:, cs] = (y * ps_ref[:, cs] * _silu(za)).astype(BF16)

    u = ub_ref[:, :conv_ch].astype(F32)
    bg = ub_ref[:, conv_ch:].astype(F32)
    cg = cz_ref[:, :conv_ch].astype(F32)
    zc = cz_ref[:, conv_ch:].astype(F32)
    v = cg * u
    vh = cg_h_ref[...].astype(F32) * u_h_ref[...].astype(F32) * keep
    ve = jnp.concatenate([vh, v], axis=0)
    y = cw_ref[CONV_WIDTH - 1: CONV_WIDTH, :] * v
    for tap in range(CONV_WIDTH - 1):
        back = CONV_WIDTH - 1 - tap
        y = y + cw_ref[tap: tap + 1, :] * pltpu.roll(ve, back, axis=0)[HALO:, :]
    yc_ref[...] = (bg * y * _silu(zc)).astype(BF16)

    a = jnp.dot(ya_ref[...], wa_ref[...], preferred_element_type=F32)
    c = jnp.dot(yc_ref[...], wc_ref[...], preferred_element_type=F32)
    merged = (g0_ref[...].astype(F32) * a + g1_ref[...].astype(F32) * b
              + g2_ref[...].astype(F32) * c)
    o_ref[...] = merged.astype(o_ref.dtype)


def _mix(pg, gates, yb, pool_w, pool_scale, conv_w, w_a, w_b, w_c, seq, tm):
    n = pg.shape[0]
    pool_width, d = w_a.shape
    sb_width = w_b.shape[0]
    conv_ch = w_c.shape[0]
    per_seq = seq // tm
    u_off = 2 * pool_width + 4 * sb_width
    wide = 2 * pool_width
    assert 2 * conv_ch == wide and u_off % wide == 0

    def halo_rows(i):
        return jnp.maximum(i * (tm // HALO) - 1, 0)

    def resident(shape):
        return pl.BlockSpec(shape, lambda i: (0,) * len(shape), pipeline_mode=pl.Buffered(1))

    in_specs = [
        pl.BlockSpec((tm, wide), lambda i: (i, 0)),
        pl.BlockSpec((tm, wide), lambda i: (i, u_off // wide)),
        pl.BlockSpec((tm, wide), lambda i: (i, u_off // wide + 1)),
        pl.BlockSpec((HALO, pool_width), lambda i: (halo_rows(i), 0)),
        pl.BlockSpec((HALO, conv_ch), lambda i: (halo_rows(i), u_off // conv_ch)),
        pl.BlockSpec((HALO, conv_ch), lambda i: (halo_rows(i), u_off // conv_ch + 2)),
        pl.BlockSpec((tm, sb_width), lambda i: (i, 0)),
        resident(pool_w.shape),
        resident((1, pool_width)),
        resident(conv_w.shape),
        resident(w_a.shape),
        resident(w_b.shape),
        resident(w_c.shape),
        pl.BlockSpec((tm, d), lambda i: (i, 0)),
        pl.BlockSpec((tm, d), lambda i: (i, 1)),
        pl.BlockSpec((tm, d), lambda i: (i, 2)),
    ]
    return pl.pallas_call(
        functools.partial(_mix_kernel, per_seq=per_seq),
        grid=(n // tm,),
        in_specs=in_specs,
        out_specs=pl.BlockSpec((tm, d), lambda i: (i, 0)),
        out_shape=jax.ShapeDtypeStruct((n, d), BF16),
        scratch_shapes=[pltpu.VMEM((tm, pool_width), BF16), pltpu.VMEM((tm, conv_ch), BF16)],
        compiler_params=_params("parallel"),
        name="mix_merge",
    )(pg, pg, pg, pg, pg, pg, yb, pool_w, pool_scale, conv_w, w_a, w_b, w_c,
      gates, gates, gates)


def _out_final_kernel(m_ref, w_ref, x_ref, rg_ref, fg_ref, o_ref):
    y = jnp.dot(m_ref[...], w_ref[...], preferred_element_type=F32)
    xn = x_ref[...] + rg_ref[0] * y
    ms = jnp.mean(xn * xn, axis=-1, keepdims=True)
    o_ref[...] = xn * lax.rsqrt(ms + RMS_EPS) * fg_ref[...]


def _out_next_kernel(m_ref, w_ref, x_ref, rg_ref, g_ref, sc_ref, sh_ref, o_ref, h_ref):
    y = jnp.dot(m_ref[...], w_ref[...], preferred_element_type=F32)
    xn = x_ref[...] + rg_ref[0] * y
    o_ref[...] = xn
    h_ref[...] = _modulated_norm(xn, g_ref, sc_ref, sh_ref)


def _out(merged, w_out, x2, res_gate3, seq, tm, final_g=None, next_norm=None):
    n, d = x2.shape
    per_seq = seq // tm
    rows = pl.BlockSpec((tm, d), lambda i: (i, 0))
    vec = pl.BlockSpec((1, d), lambda i: (0, 0))
    per_batch = pl.BlockSpec((1, 1, d), lambda i: (i // per_seq, 0, 0))
    in_specs = [rows, pl.BlockSpec((d, d), lambda i: (0, 0)), rows, per_batch]
    args = [merged, w_out, x2, res_gate3]
    if final_g is not None:
        return pl.pallas_call(
            _out_final_kernel,
            grid=(n // tm,),
            in_specs=in_specs + [vec],
            out_specs=rows,
            out_shape=jax.ShapeDtypeStruct((n, d), F32),
            compiler_params=_params("parallel"),
            name="out_proj_final",
        )(*args, final_g)
    return pl.pallas_call(
        _out_next_kernel,
        grid=(n // tm,),
        in_specs=in_specs + [vec, per_batch, per_batch],
        out_specs=[rows, rows],
        out_shape=[jax.ShapeDtypeStruct((n, d), F32), jax.ShapeDtypeStruct((n, d), BF16)],
        compiler_params=_params("parallel"),
        name="out_proj",
    )(*args, *next_norm)


def kernel(x, c, norm_g, w_ada, b_ada, w_in, pool_w, pool_scale, conv_w, w_br_a, w_br_b,
           w_br_c, w_gate, b_gate, w_out, final_g):
    batch, seq, d = x.shape
    depth = w_in.shape[0]
    pool_width = w_br_a.shape[1]
    sb_width = w_br_b.shape[1]
    q_off = 2 * pool_width
    n = batch * seq
    tm = min(512, seq)

    c_pad = jnp.pad(c, ((0, -batch % 8), (0, 0)))
    mod = _ada(c_pad, w_ada, b_ada)[:, :batch]
    x2 = x.reshape(n, d)

    def norm_inputs(l):
        return (norm_g[l].reshape(1, d), mod[l, :, d:2 * d].reshape(batch, 1, d),
                mod[l, :, :d].reshape(batch, 1, d))

    in_dim = w_in.shape[2]
    q_cols = (jnp.arange(in_dim) >= q_off) & (jnp.arange(in_dim) < q_off + sb_width)
    in_scale = jnp.where(q_cols, HEAD_DIM ** -0.5, 1.0).astype(F32).reshape(1, in_dim)
    gate_scale = jnp.ones((1, w_gate.shape[2]), F32)

    h = _norm_mod(x2, *norm_inputs(0), seq, tm)
    for l in range(depth):
        res_gate3 = mod[l, :, 2 * d:].reshape(batch, 1, d)
        pg = _proj(h, w_in[l], in_scale, None, tm, PROJ_TN)
        gates = _proj(h, w_gate[l], gate_scale, b_gate[l].reshape(1, -1), tm, PROJ_TN)
        yb = _attention(pg, batch, seq, sb_width, q_off)
        merged = _mix(pg, gates, yb, pool_w[l].astype(BF16), pool_scale[l].reshape(1, -1),
                      conv_w[l], w_br_a[l].astype(BF16), w_br_b[l].astype(BF16),
                      w_br_c[l].astype(BF16), seq, min(256, seq))
        if l == depth - 1:
            x2 = _out(merged, w_out[l].astype(BF16), x2, res_gate3, seq, tm,
                      final_g=final_g.reshape(1, d))
        else:
            x2, h = _out(merged, w_out[l].astype(BF16), x2, res_gate3, seq, tm,
                         next_norm=norm_inputs(l + 1))

    return x2.reshape(batch, seq, d)
```

```python
import functools

import jax
import jax.numpy as jnp
from jax import lax
from jax.experimental import pallas as pl
from jax.experimental.pallas import tpu as pltpu

RMS_EPS = 1e-6
HEAD_DIM = 128
POOL_WINDOWS = (2, 4, 8, 16)
CONV_WIDTH = 3
HALO = 16
VMEM_LIMIT_BYTES = 56 * 1024 * 1024
PROJ_TN = 2048

F32 = jnp.float32
BF16 = jnp.bfloat16


def _params(*sem):
    return pltpu.CompilerParams(dimension_semantics=sem, vmem_limit_bytes=VMEM_LIMIT_BYTES)


def _sigmoid(v):
    return 0.5 * jnp.tanh(0.5 * v) + 0.5


def _silu(v):
    return v * _sigmoid(v)


def _ada_kernel(c_ref, w_ref, b_ref, o_ref):
    s = _silu(c_ref[...])
    o_ref[0] = jnp.dot(s, w_ref[0], precision=lax.Precision.HIGHEST,
                       preferred_element_type=F32) + b_ref[0]


def _ada(c_pad, w_ada, b_ada, tn=768):
    depth, d, n3 = w_ada.shape
    rows = c_pad.shape[0]
    return pl.pallas_call(
        _ada_kernel,
        grid=(depth, n3 // tn),
        in_specs=[
            pl.BlockSpec((rows, d), lambda l, j: (0, 0)),
            pl.BlockSpec((1, d, tn), lambda l, j: (l, 0, j)),
            pl.BlockSpec((1, 1, tn), lambda l, j: (l, 0, j)),
        ],
        out_specs=pl.BlockSpec((1, rows, tn), lambda l, j: (l, 0, j)),
        out_shape=jax.ShapeDtypeStruct((depth, rows, n3), F32),
        compiler_params=_params("parallel", "parallel"),
        name="ada_mod",
    )(c_pad, w_ada, b_ada.reshape(depth, 1, n3))


CAST_ROWS = 64


def _modulated_norm(x, g_ref, sc_ref, sh_ref):
    ms = jnp.mean(x * x, axis=-1, keepdims=True)
    y = x * lax.rsqrt(ms + RMS_EPS) * g_ref[...]
    return (y * (1.0 + sc_ref[0]) + sh_ref[0]).astype(BF16)


def _norm_mod_kernel(x_ref, g_ref, sc_ref, sh_ref, h_ref):
    h_ref[...] = _modulated_norm(x_ref[...], g_ref, sc_ref, sh_ref)


def _norm_mod(x2, g, scale3, shift3, seq, tm):
    n, d = x2.shape
    per_seq = seq // tm
    return pl.pallas_call(
        _norm_mod_kernel,
        grid=(n // tm,),
        in_specs=[
            pl.BlockSpec((tm, d), lambda i: (i, 0)),
            pl.BlockSpec((1, d), lambda i: (0, 0)),
            pl.BlockSpec((1, 1, d), lambda i: (i // per_seq, 0, 0)),
            pl.BlockSpec((1, 1, d), lambda i: (i // per_seq, 0, 0)),
        ],
        out_specs=pl.BlockSpec((tm, d), lambda i: (i, 0)),
        out_shape=jax.ShapeDtypeStruct((n, d), BF16),
        compiler_params=_params("parallel"),
        name="norm_mod",
    )(x2, g, scale3, shift3)


def _proj_kernel(h_ref, w_ref, cs_ref, *rest, gated):
    if gated:
        b_ref, o_ref, wb_ref = rest
    else:
        o_ref, wb_ref = rest

    @pl.when(pl.program_id(1) == 0)
    def _():
        cs = cs_ref[...]

        def body(r, carry):
            rows = pl.ds(pl.multiple_of(r * CAST_ROWS, CAST_ROWS), CAST_ROWS)
            wb_ref[rows, :] = (w_ref[rows, :] * cs).astype(BF16)
            return carry

        lax.fori_loop(0, w_ref.shape[0] // CAST_ROWS, body, 0)

    acc = jnp.dot(h_ref[...], wb_ref[...], preferred_element_type=F32)
    if gated:
        acc = _sigmoid(acc + b_ref[...])
    o_ref[...] = acc.astype(o_ref.dtype)


def _proj(h, w_stack, layer, col_scale, bias, tm, tn):
    n, d = h.shape
    ncol = w_stack.shape[2]
    gated = bias is not None
    in_specs = [
        pl.BlockSpec((tm, d), lambda j, i: (i, 0)),
        pl.BlockSpec((None, d, tn), lambda j, i: (layer, 0, j)),
        pl.BlockSpec((1, tn), lambda j, i: (0, j)),
    ]
    args = [h, w_stack, col_scale]
    if gated:
        in_specs.append(pl.BlockSpec((1, tn), lambda j, i: (0, j)))
        args.append(bias)
    return pl.pallas_call(
        functools.partial(_proj_kernel, gated=gated),
        grid=(ncol // tn, n // tm),
        in_specs=in_specs,
        out_specs=pl.BlockSpec((tm, tn), lambda j, i: (i, j)),
        out_shape=jax.ShapeDtypeStruct((n, ncol), BF16),
        scratch_shapes=[pltpu.VMEM((d, tn), BF16)],
        compiler_params=_params("arbitrary", "arbitrary"),
        name="gate_proj" if gated else "in_proj",
    )(*args)


LOG2E = 1.4426950408889634
DEAD_CARRY = 88.0
MASKED_LOGIT = -1e30


def _attn_kernel(q_ref, k_ref, v_ref, zb_ref, o_ref, *, rows, unroll, wave):
    seq = q_ref.shape[0]
    win = 2 * rows
    col = lax.broadcasted_iota(jnp.int32, (rows, win), 1)
    ahead = col - lax.broadcasted_iota(jnp.int32, (rows, win), 0)
    suffix = (lax.broadcasted_iota(jnp.int32, (win, win), 0)
              > lax.broadcasted_iota(jnp.int32, (win, win), 1)).astype(BF16)

    def chunk_start(top):
        return pl.multiple_of(jnp.maximum(top - win, 0), rows)

    def scores(q, start, visible):
        z = lax.dot_general(q, k_ref[pl.ds(start, win), :], (((1,), (1,)), ((), ())),
                            preferred_element_type=F32)
        z = jnp.where(visible, z, MASKED_LOGIT)
        return z, jnp.maximum(z, 0.0) + jnp.log(1.0 + jnp.exp2(jnp.abs(z) * -LOG2E))

    def later_sums(sp):
        return jnp.dot(sp.astype(BF16), suffix, preferred_element_type=F32)

    def weighted(z, sp, between, start):
        a = jnp.exp2((z - sp - between) * LOG2E)
        return jnp.dot(a.astype(BF16), v_ref[pl.ds(start, win), :], preferred_element_type=F32)

    def row_sums(sp, between):
        return between[:, 0:1] + sp[:, 0:1]

    def group_rows(g):
        return pl.ds(pl.multiple_of(g * rows, rows), rows)

    def step(it, _):
        groups = [it * unroll + u for u in range(unroll)]
        qs = [q_ref[group_rows(g), :] for g in groups]
        starts1 = [chunk_start((g + 1) * rows) for g in groups]
        starts2 = [chunk_start(s1) for s1 in starts1]
        starts = starts1 + starts2
        visible = ([ahead < g * rows - s1 for g, s1 in zip(groups, starts1)]
                   + [col < s1 - s2 for s1, s2 in zip(starts1, starts2)])
        n_chunks = 2 * unroll
        zs, betweens, pvs = [None] * n_chunks, [None] * n_chunks, [None] * n_chunks
        for w in range(n_chunks // wave + 2):
            for c in range(w * wave, min((w + 1) * wave, n_chunks)):
                zs[c] = scores(qs[c % unroll], starts[c], visible[c])
            for c in range(max(w - 1, 0) * wave, min(w * wave, n_chunks)):
                betweens[c] = later_sums(zs[c][1])
            for c in range(max(w - 2, 0) * wave, min((w - 1) * wave, n_chunks)):
                pvs[c] = weighted(zs[c][0], zs[c][1], betweens[c], starts[c])
        sums = [row_sums(sp, b) for (_, sp), b in zip(zs, betweens)]

        accs = [pvs[u] + jnp.exp(-sums[u]) * pvs[unroll + u] for u in range(unroll)]
        carries = [sums[u] + sums[unroll + u] for u in range(unroll)]
        alive = [jnp.min(carry) < DEAD_CARRY for carry in carries]

        for u, g in enumerate(groups):
            def sweep_earlier(q=qs[u], acc=accs[u], carry=carries[u], top=starts2[u]):
                def live(state):
                    top, carry, _ = state
                    return jnp.logical_and(top > 0, jnp.min(carry) < DEAD_CARRY)

                def earlier_chunk(state):
                    top, carry, acc = state
                    start = chunk_start(top)
                    z, sp = scores(q, start, col < top - start)
                    between = later_sums(sp)
                    pv = weighted(z, sp, between, start)
                    return start, carry + row_sums(sp, between), acc + jnp.exp(-carry) * pv

                return lax.while_loop(live, earlier_chunk, (top, carry, acc))[2]

            acc = lax.cond(alive[u], sweep_earlier, lambda acc=accs[u]: acc)
            zb = zb_ref[group_rows(g), :].astype(F32)
            o_ref[group_rows(g), :] = (acc * _silu(zb)).astype(o_ref.dtype)
        return 0

    lax.fori_loop(0, seq // (rows * unroll), step, 0)


def _attention(pg, batch, seq, sb_width, q_off, rows=128, unroll=8, wave=4):
    n = pg.shape[0]
    heads = sb_width // HEAD_DIM
    c0 = q_off // HEAD_DIM
    assert seq % (rows * unroll) == 0 and seq >= 2 * rows

    def spec(section):
        return pl.BlockSpec((seq, HEAD_DIM), lambda b, h: (b, c0 + section * heads + h))

    return pl.pallas_call(
        functools.partial(_attn_kernel, rows=rows, unroll=unroll, wave=wave),
        grid=(batch, heads),
        in_specs=[spec(0), spec(1), spec(2), spec(3)],
        out_specs=pl.BlockSpec((seq, HEAD_DIM), lambda b, h: (b, h)),
        out_shape=jax.ShapeDtypeStruct((n, sb_width), BF16),
        compiler_params=_params("parallel", "parallel"),
        name="stickbreak_attn",
    )(pg, pg, pg, pg)


def _mix_kernel(xz_ref, ub_ref, cz_ref, xa_h_ref, u_h_ref, cg_h_ref, yb_ref,
                pw_ref, ps_ref, cw_ref, wa_ref, wb_ref, wc_ref, g0_ref, g1_ref, g2_ref,
                o_ref, ya_ref, yc_ref, *, per_seq):
    tm = xz_ref.shape[0]
    pool_width = xz_ref.shape[1] // 2
    conv_ch = ub_ref.shape[1] // 2
    group = pool_width // len(POOL_WINDOWS)
    i = pl.program_id(0)

    seq_start = (i % per_seq) == 0
    keep = jnp.where(seq_start, 0.0, 1.0)
    pos = (i % per_seq) * tm + lax.broadcasted_iota(jnp.int32, (tm, 1), 0)

    def pool_group(g, w):
        cs = slice(g * group, (g + 1) * group)
        xa = xz_ref[:, cs].astype(F32)
        halo = xa_h_ref[:, cs].astype(F32) * keep
        s = jnp.concatenate([halo, xa], axis=0)
        shift = 1
        while shift < w:
            s = s + pltpu.roll(s, shift, axis=0)
            shift *= 2
        count = jnp.minimum(pos + 1, w).astype(F32)
        mixed = s[HALO:, :] / count - xa
        y = jnp.dot(mixed.astype(BF16), pw_ref[g], preferred_element_type=F32)
        za = xz_ref[:, pool_width + g * group: pool_width + (g + 1) * group].astype(F32)
        ya_ref[:, cs] = (y * ps_ref[:, cs] * _silu(za)).astype(BF16)

    def conv_channels(cs):
        gate_cs = slice(conv_ch + cs.start, conv_ch + cs.stop)
        v = cz_ref[:, cs].astype(F32) * ub_ref[:, cs].astype(F32)
        vh = cg_h_ref[:, cs].astype(F32) * u_h_ref[:, cs].astype(F32) * keep
        ve = jnp.concatenate([vh, v], axis=0)
        y = cw_ref[CONV_WIDTH - 1: CONV_WIDTH, cs] * v
        for tap in range(CONV_WIDTH - 1):
            back = CONV_WIDTH - 1 - tap
            y = y + cw_ref[tap: tap + 1, cs] * pltpu.roll(ve, back, axis=0)[HALO:, :]
        yc_ref[:, cs] = (ub_ref[:, gate_cs].astype(F32) * y
                         * _silu(cz_ref[:, gate_cs].astype(F32))).astype(BF16)

    pieces = [functools.partial(pool_group, g, w) for g, w in enumerate(POOL_WINDOWS)]
    pieces += [functools.partial(conv_channels, slice(k * group, (k + 1) * group))
               for k in range(conv_ch // group)]
    slab = wb_ref.shape[1] // len(pieces)
    b_slabs = []
    for k, piece in enumerate(pieces):
        b_slabs.append(jnp.dot(yb_ref[...], wb_ref[:, k * slab:(k + 1) * slab],
                               preferred_element_type=F32))
        piece()
    b = jnp.concatenate(b_slabs, axis=1)
    a = jnp.dot(ya_ref[...], wa_ref[...], preferred_element_type=F32)
    c = jnp.dot(yc_ref[...], wc_ref[...], preferred_element_type=F32)
    merged = (g0_ref[...].astype(F32) * a + g1_ref[...].astype(F32) * b
              + g2_ref[...].astype(F32) * c)
    o_ref[...] = merged.astype(o_ref.dtype)


def _mix(pg, gates, yb, layer, pool_w, pool_scale, conv_w, w_a, w_b, w_c, seq, tm):
    n = pg.shape[0]
    _, pool_width, d = w_a.shape
    sb_width = w_b.shape[1]
    conv_ch = w_c.shape[1]
    per_seq = seq // tm
    u_off = 2 * pool_width + 4 * sb_width
    wide = 2 * pool_width
    assert 2 * conv_ch == wide and u_off % wide == 0

    def halo_rows(i):
        return jnp.maximum(i * (tm // HALO) - 1, 0)

    def resident(stack_shape):
        tail = stack_shape[1:]
        return pl.BlockSpec((None,) + tail, lambda i: (layer,) + (0,) * len(tail),
                            pipeline_mode=pl.Buffered(1))

    in_specs = [
        pl.BlockSpec((tm, wide), lambda i: (i, 0)),
        pl.BlockSpec((tm, wide), lambda i: (i, u_off // wide)),
        pl.BlockSpec((tm, wide), lambda i: (i, u_off // wide + 1)),
        pl.BlockSpec((HALO, pool_width), lambda i: (halo_rows(i), 0)),
        pl.BlockSpec((HALO, conv_ch), lambda i: (halo_rows(i), u_off // conv_ch)),
        pl.BlockSpec((HALO, conv_ch), lambda i: (halo_rows(i), u_off // conv_ch + 2)),
        pl.BlockSpec((tm, sb_width), lambda i: (i, 0)),
        resident(pool_w.shape),
        resident(pool_scale.shape),
        resident(conv_w.shape),
        resident(w_a.shape),
        resident(w_b.shape),
        resident(w_c.shape),
        pl.BlockSpec((tm, d), lambda i: (i, 0)),
        pl.BlockSpec((tm, d), lambda i: (i, 1)),
        pl.BlockSpec((tm, d), lambda i: (i, 2)),
    ]
    return pl.pallas_call(
        functools.partial(_mix_kernel, per_seq=per_seq),
        grid=(n // tm,),
        in_specs=in_specs,
        out_specs=pl.BlockSpec((tm, d), lambda i: (i, 0)),
        out_shape=jax.ShapeDtypeStruct((n, d), BF16),
        scratch_shapes=[pltpu.VMEM((tm, pool_width), BF16), pltpu.VMEM((tm, conv_ch), BF16)],
        compiler_params=_params("parallel"),
        name="mix_merge",
    )(pg, pg, pg, pg, pg, pg, yb, pool_w, pool_scale, conv_w, w_a, w_b, w_c,
      gates, gates, gates)


def _out_final_kernel(m_ref, w_ref, x_ref, rg_ref, fg_ref, o_ref):
    y = jnp.dot(m_ref[...], w_ref[...], preferred_element_type=F32)
    xn = x_ref[...] + rg_ref[0] * y
    ms = jnp.mean(xn * xn, axis=-1, keepdims=True)
    o_ref[...] = xn * lax.rsqrt(ms + RMS_EPS) * fg_ref[...]


def _out_next_kernel(m_ref, w_ref, x_ref, rg_ref, g_ref, sc_ref, sh_ref, o_ref, h_ref):
    y = jnp.dot(m_ref[...], w_ref[...], preferred_element_type=F32)
    xn = x_ref[...] + rg_ref[0] * y
    o_ref[...] = xn
    h_ref[...] = _modulated_norm(xn, g_ref, sc_ref, sh_ref)


def _out(merged, w_out, layer, x2, res_gate3, seq, tm, final_g=None, next_norm=None):
    n, d = x2.shape
    per_seq = seq // tm
    rows = pl.BlockSpec((tm, d), lambda i: (i, 0))
    vec = pl.BlockSpec((1, d), lambda i: (0, 0))
    per_batch = pl.BlockSpec((1, 1, d), lambda i: (i // per_seq, 0, 0))
    in_specs = [rows, pl.BlockSpec((None, d, d), lambda i: (layer, 0, 0)), rows, per_batch]
    args = [merged, w_out, x2, res_gate3]
    if final_g is not None:
        return pl.pallas_call(
            _out_final_kernel,
            grid=(n // tm,),
            in_specs=in_specs + [vec],
            out_specs=rows,
            out_shape=jax.ShapeDtypeStruct((n, d), F32),
            compiler_params=_params("parallel"),
            name="out_proj_final",
        )(*args, final_g)
    return pl.pallas_call(
        _out_next_kernel,
        grid=(n // tm,),
        in_specs=in_specs + [vec, per_batch, per_batch],
        out_specs=[rows, rows],
        out_shape=[jax.ShapeDtypeStruct((n, d), F32), jax.ShapeDtypeStruct((n, d), BF16)],
        compiler_params=_params("parallel"),
        name="out_proj",
    )(*args, *next_norm)


def kernel(x, c, norm_g, w_ada, b_ada, w_in, pool_w, pool_scale, conv_w, w_br_a, w_br_b,
           w_br_c, w_gate, b_gate, w_out, final_g):
    batch, seq, d = x.shape
    depth = w_in.shape[0]
    pool_width = w_br_a.shape[1]
    sb_width = w_br_b.shape[1]
    q_off = 2 * pool_width
    n = batch * seq
    tm = min(512, seq)

    c_pad = jnp.pad(c, ((0, -batch % 8), (0, 0)))
    mod = _ada(c_pad, w_ada, b_ada)[:, :batch]
    x2 = x.reshape(n, d)

    def norm_inputs(l):
        return (norm_g[l].reshape(1, d), mod[l, :, d:2 * d].reshape(batch, 1, d),
                mod[l, :, :d].reshape(batch, 1, d))

    in_dim = w_in.shape[2]
    q_cols = (jnp.arange(in_dim) >= q_off) & (jnp.arange(in_dim) < q_off + sb_width)
    in_scale = jnp.where(q_cols, HEAD_DIM ** -0.5, 1.0).astype(F32).reshape(1, in_dim)
    gate_scale = jnp.ones((1, w_gate.shape[2]), F32)

    mix_params = (pool_w.astype(BF16), pool_scale.reshape(depth, 1, pool_width), conv_w,
                  w_br_a.astype(BF16), w_br_b.astype(BF16), w_br_c.astype(BF16))
    w_out_bf = w_out.astype(BF16)

    h = _norm_mod(x2, *norm_inputs(0), seq, tm)
    for l in range(depth):
        res_gate3 = mod[l, :, 2 * d:].reshape(batch, 1, d)
        pg = _proj(h, w_in, l, in_scale, None, tm, PROJ_TN)
        gates = _proj(h, w_gate, l, gate_scale, b_gate[l].reshape(1, -1), tm, PROJ_TN)
        yb = _attention(pg, batch, seq, sb_width, q_off)
        merged = _mix(pg, gates, yb, l, *mix_params, seq, min(256, seq))
        if l == depth - 1:
            x2 = _out(merged, w_out_bf, l, x2, res_gate3, seq, tm,
                      final_g=final_g.reshape(1, d))
        else:
            x2, h = _out(merged, w_out_bf, l, x2, res_gate3, seq, tm,
                         next_norm=norm_inputs(l + 1))

    return x2.reshape(batch, seq, d)
```

```python
import functools

import jax
import jax.numpy as jnp
from jax import lax
from jax.experimental import pallas as pl
from jax.experimental.pallas import tpu as pltpu

RMS_EPS = 1e-6
HEAD_DIM = 128
POOL_WINDOWS = (2, 4, 8, 16)
CONV_WIDTH = 3
HALO = 16
VMEM_LIMIT_BYTES = 56 * 1024 * 1024
PROJ_TN = 2048

F32 = jnp.float32
BF16 = jnp.bfloat16


def _params(*sem):
    return pltpu.CompilerParams(dimension_semantics=sem, vmem_limit_bytes=VMEM_LIMIT_BYTES)


def _sigmoid(v):
    return 0.5 * jnp.tanh(0.5 * v) + 0.5


def _silu(v):
    return v * _sigmoid(v)


def _ada_kernel(c_ref, w_ref, b_ref, o_ref):
    s = _silu(c_ref[...])
    o_ref[0] = jnp.dot(s, w_ref[0], precision=lax.Precision.HIGHEST,
                       preferred_element_type=F32) + b_ref[0]


def _ada(c_pad, w_ada, b_ada, tn=1536):
    depth, d, n3 = w_ada.shape
    rows = c_pad.shape[0]
    return pl.pallas_call(
        _ada_kernel,
        grid=(depth, n3 // tn),
        in_specs=[
            pl.BlockSpec((rows, d), lambda l, j: (0, 0)),
            pl.BlockSpec((1, d, tn), lambda l, j: (l, 0, j)),
            pl.BlockSpec((1, 1, tn), lambda l, j: (l, 0, j)),
        ],
        out_specs=pl.BlockSpec((1, rows, tn), lambda l, j: (l, 0, j)),
        out_shape=jax.ShapeDtypeStruct((depth, rows, n3), F32),
        compiler_params=_params("parallel", "parallel"),
        name="ada_mod",
    )(c_pad, w_ada, b_ada.reshape(depth, 1, n3))


CAST_ROWS = 64


def _modulated_norm(x, g_ref, sc_ref, sh_ref):
    ms = jnp.mean(x * x, axis=-1, keepdims=True)
    y = x * lax.rsqrt(ms + RMS_EPS) * g_ref[...]
    return (y * (1.0 + sc_ref[0]) + sh_ref[0]).astype(BF16)


def _norm_mod_kernel(x_ref, g_ref, sc_ref, sh_ref, h_ref):
    h_ref[...] = _modulated_norm(x_ref[...], g_ref, sc_ref, sh_ref)


def _norm_mod(x2, g, scale3, shift3, seq, tm):
    n, d = x2.shape
    per_seq = seq // tm
    return pl.pallas_call(
        _norm_mod_kernel,
        grid=(n // tm,),
        in_specs=[
            pl.BlockSpec((tm, d), lambda i: (i, 0)),
            pl.BlockSpec((1, d), lambda i: (0, 0)),
            pl.BlockSpec((1, 1, d), lambda i: (i // per_seq, 0, 0)),
            pl.BlockSpec((1, 1, d), lambda i: (i // per_seq, 0, 0)),
        ],
        out_specs=pl.BlockSpec((tm, d), lambda i: (i, 0)),
        out_shape=jax.ShapeDtypeStruct((n, d), BF16),
        compiler_params=_params("parallel"),
        name="norm_mod",
    )(x2, g, scale3, shift3)


def _proj_kernel(h_ref, w_ref, cs_ref, *rest, gated):
    if gated:
        b_ref, o_ref, wb_ref = rest
    else:
        o_ref, wb_ref = rest

    @pl.when(pl.program_id(1) == 0)
    def _():
        cs = cs_ref[...]

        def body(r, carry):
            rows = pl.ds(pl.multiple_of(r * CAST_ROWS, CAST_ROWS), CAST_ROWS)
            wb_ref[rows, :] = (w_ref[rows, :] * cs).astype(BF16)
            return carry

        lax.fori_loop(0, w_ref.shape[0] // CAST_ROWS, body, 0)

    acc = jnp.dot(h_ref[...], wb_ref[...], preferred_element_type=F32)
    if gated:
        acc = _sigmoid(acc + b_ref[...])
    o_ref[...] = acc.astype(o_ref.dtype)


def _proj(h, w_stack, layer, col_scale, bias, tm, tn):
    n, d = h.shape
    ncol = w_stack.shape[2]
    gated = bias is not None
    in_specs = [
        pl.BlockSpec((tm, d), lambda j, i: (i, 0)),
        pl.BlockSpec((None, d, tn), lambda j, i: (layer, 0, j)),
        pl.BlockSpec((1, tn), lambda j, i: (0, j)),
    ]
    args = [h, w_stack, col_scale]
    if gated:
        in_specs.append(pl.BlockSpec((1, tn), lambda j, i: (0, j)))
        args.append(bias)
    return pl.pallas_call(
        functools.partial(_proj_kernel, gated=gated),
        grid=(ncol // tn, n // tm),
        in_specs=in_specs,
        out_specs=pl.BlockSpec((tm, tn), lambda j, i: (i, j)),
        out_shape=jax.ShapeDtypeStruct((n, ncol), BF16),
        scratch_shapes=[pltpu.VMEM((d, tn), BF16)],
        compiler_params=_params("arbitrary", "arbitrary"),
        name="gate_proj" if gated else "in_proj",
    )(*args)


LOG2E = 1.4426950408889634
DEAD_CARRY = 88.0
MASKED_LOGIT = -1e30


def _attn_kernel(q_ref, k_ref, v_ref, zb_ref, o_ref, *, rows, unroll, wave):
    seq = q_ref.shape[0]
    win = 2 * rows
    col = lax.broadcasted_iota(jnp.int32, (rows, win), 1)
    ahead = col - lax.broadcasted_iota(jnp.int32, (rows, win), 0)
    suffix = (lax.broadcasted_iota(jnp.int32, (win, win), 0)
              > lax.broadcasted_iota(jnp.int32, (win, win), 1)).astype(BF16)

    def chunk_start(top):
        return pl.multiple_of(jnp.maximum(top - win, 0), rows)

    def scores(q, start, visible):
        z = lax.dot_general(q, k_ref[pl.ds(start, win), :], (((1,), (1,)), ((), ())),
                            preferred_element_type=F32)
        if visible is not None:
            z = jnp.where(visible, z, MASKED_LOGIT)
        return z, jnp.maximum(z, 0.0) + jnp.log(1.0 + jnp.exp2(jnp.abs(z) * -LOG2E))

    def later_sums(sp):
        return jnp.dot(sp.astype(BF16), suffix, preferred_element_type=F32)

    def weighted(z, sp, between, start):
        a = jnp.exp2((z - sp - between) * LOG2E)
        return jnp.dot(a.astype(BF16), v_ref[pl.ds(start, win), :], preferred_element_type=F32)

    def row_sums(sp, between):
        return between[:, 0:1] + sp[:, 0:1]

    def group_rows(g):
        return pl.ds(pl.multiple_of(g * rows, rows), rows)

    def step(it, _, near_start):
        groups = [it * unroll + u for u in range(unroll)]
        qs = [q_ref[group_rows(g), :] for g in groups]
        starts1 = [chunk_start((g + 1) * rows) for g in groups]
        starts2 = [chunk_start(s1) for s1 in starts1]
        starts = starts1 + starts2
        if near_start:
            visible = ([ahead < g * rows - s1 for g, s1 in zip(groups, starts1)]
                       + [col < s1 - s2 for s1, s2 in zip(starts1, starts2)])
        else:
            visible = [ahead < rows] * unroll + [None] * unroll
        n_chunks = 2 * unroll
        zs, betweens, pvs = [None] * n_chunks, [None] * n_chunks, [None] * n_chunks
        for w in range(n_chunks // wave + 2):
            for c in range(w * wave, min((w + 1) * wave, n_chunks)):
                zs[c] = scores(qs[c % unroll], starts[c], visible[c])
            for c in range(max(w - 1, 0) * wave, min(w * wave, n_chunks)):
                betweens[c] = later_sums(zs[c][1])
            for c in range(max(w - 2, 0) * wave, min((w - 1) * wave, n_chunks)):
                pvs[c] = weighted(zs[c][0], zs[c][1], betweens[c], starts[c])
        sums = [row_sums(sp, b) for (_, sp), b in zip(zs, betweens)]

        accs = [pvs[u] + jnp.exp(-sums[u]) * pvs[unroll + u] for u in range(unroll)]
        carries = [sums[u] + sums[unroll + u] for u in range(unroll)]
        alive = [jnp.min(carry) < DEAD_CARRY for carry in carries]

        for u, g in enumerate(groups):
            def sweep_earlier(q=qs[u], acc=accs[u], carry=carries[u], top=starts2[u]):
                def live(state):
                    top, carry, _ = state
                    return jnp.logical_and(top > 0, jnp.min(carry) < DEAD_CARRY)

                def earlier_chunk(state):
                    top, carry, acc = state
                    start = chunk_start(top)
                    z, sp = scores(q, start, col < top - start)
                    between = later_sums(sp)
                    pv = weighted(z, sp, between, start)
                    return start, carry + row_sums(sp, between), acc + jnp.exp(-carry) * pv

                return lax.while_loop(live, earlier_chunk, (top, carry, acc))[2]

            acc = lax.cond(alive[u], sweep_earlier, lambda acc=accs[u]: acc)
            zb = zb_ref[group_rows(g), :].astype(F32)
            o_ref[group_rows(g), :] = (acc * _silu(zb)).astype(o_ref.dtype)
        return 0

    assert unroll >= 3
    step(0, 0, True)
    lax.fori_loop(1, seq // (rows * unroll), functools.partial(step, near_start=False), 0)


def _attention(pg, batch, seq, sb_width, q_off, rows=128, unroll=8, wave=4):
    n = pg.shape[0]
    heads = sb_width // HEAD_DIM
    c0 = q_off // HEAD_DIM
    assert seq % (rows * unroll) == 0 and seq >= 2 * rows

    def spec(section):
        return pl.BlockSpec((seq, HEAD_DIM), lambda b, h: (b, c0 + section * heads + h))

    return pl.pallas_call(
        functools.partial(_attn_kernel, rows=rows, unroll=unroll, wave=wave),
        grid=(batch, heads),
        in_specs=[spec(0), spec(1), spec(2), spec(3)],
        out_specs=pl.BlockSpec((seq, HEAD_DIM), lambda b, h: (b, h)),
        out_shape=jax.ShapeDtypeStruct((n, sb_width), BF16),
        compiler_params=_params("parallel", "parallel"),
        name="stickbreak_attn",
    )(pg, pg, pg, pg)


def _mix_kernel(xz_ref, ub_ref, cz_ref, xa_h_ref, u_h_ref, cg_h_ref, yb_ref,
                pw_ref, ps_ref, cw_ref, wa_ref, wb_ref, wc_ref, g0_ref, g1_ref, g2_ref,
                o_ref, ya_ref, yc_ref, *, per_seq):
    tm = xz_ref.shape[0]
    pool_width = xz_ref.shape[1] // 2
    conv_ch = ub_ref.shape[1] // 2
    group = pool_width // len(POOL_WINDOWS)
    i = pl.program_id(0)

    seq_start = (i % per_seq) == 0
    keep = jnp.where(seq_start, 0.0, 1.0)
    pos = (i % per_seq) * tm + lax.broadcasted_iota(jnp.int32, (tm, 1), 0)

    def pool_group(g, w):
        cs = slice(g * group, (g + 1) * group)
        xa = xz_ref[:, cs].astype(F32)
        halo = xa_h_ref[:, cs].astype(F32) * keep
        s = jnp.concatenate([halo, xa], axis=0)
        shift = 1
        while shift < w:
            s = s + pltpu.roll(s, shift, axis=0)
            shift *= 2
        count = jnp.minimum(pos + 1, w).astype(F32)
        mixed = s[HALO:, :] / count - xa
        y = jnp.dot(mixed.astype(BF16), pw_ref[g], preferred_element_type=F32)
        za = xz_ref[:, pool_width + g * group: pool_width + (g + 1) * group].astype(F32)
        ya_ref[:, cs] = (y * ps_ref[:, cs] * _silu(za)).astype(BF16)

    def conv_channels(cs):
        gate_cs = slice(conv_ch + cs.start, conv_ch + cs.stop)
        v = cz_ref[:, cs].astype(F32) * ub_ref[:, cs].astype(F32)
        vh = cg_h_ref[:, cs].astype(F32) * u_h_ref[:, cs].astype(F32) * keep
        ve = jnp.concatenate([vh, v], axis=0)
        y = cw_ref[CONV_WIDTH - 1: CONV_WIDTH, cs] * v
        for tap in range(CONV_WIDTH - 1):
            back = CONV_WIDTH - 1 - tap
            y = y + cw_ref[tap: tap + 1, cs] * pltpu.roll(ve, back, axis=0)[HALO:, :]
        yc_ref[:, cs] = (ub_ref[:, gate_cs].astype(F32) * y
                         * _silu(cz_ref[:, gate_cs].astype(F32))).astype(BF16)

    pieces = [functools.partial(pool_group, g, w) for g, w in enumerate(POOL_WINDOWS)]
    pieces += [functools.partial(conv_channels, slice(k * group, (k + 1) * group))
               for k in range(conv_ch // group)]
    slab = wb_ref.shape[1] // len(pieces)
    b_slabs = []
    for k, piece in enumerate(pieces):
        b_slabs.append(jnp.dot(yb_ref[...], wb_ref[:, k * slab:(k + 1) * slab],
                               preferred_element_type=F32))
        piece()
    b = jnp.concatenate(b_slabs, axis=1)
    a = jnp.dot(ya_ref[...], wa_ref[...], preferred_element_type=F32)
    c = jnp.dot(yc_ref[...], wc_ref[...], preferred_element_type=F32)
    merged = (g0_ref[...].astype(F32) * a + g1_ref[...].astype(F32) * b
              + g2_ref[...].astype(F32) * c)
    o_ref[...] = merged.astype(o_ref.dtype)


def _mix(pg, gates, yb, layer, pool_w, pool_scale, conv_w, w_a, w_b, w_c, seq, tm):
    n = pg.shape[0]
    _, pool_width, d = w_a.shape
    sb_width = w_b.shape[1]
    conv_ch = w_c.shape[1]
    per_seq = seq // tm
    u_off = 2 * pool_width + 4 * sb_width
    wide = 2 * pool_width
    assert 2 * conv_ch == wide and u_off % wide == 0

    def halo_rows(i):
        return jnp.maximum(i * (tm // HALO) - 1, 0)

    def resident(stack_shape):
        tail = stack_shape[1:]
        return pl.BlockSpec((None,) + tail, lambda i: (layer,) + (0,) * len(tail),
                            pipeline_mode=pl.Buffered(1))

    in_specs = [
        pl.BlockSpec((tm, wide), lambda i: (i, 0)),
        pl.BlockSpec((tm, wide), lambda i: (i, u_off // wide)),
        pl.BlockSpec((tm, wide), lambda i: (i, u_off // wide + 1)),
        pl.BlockSpec((HALO, pool_width), lambda i: (halo_rows(i), 0)),
        pl.BlockSpec((HALO, conv_ch), lambda i: (halo_rows(i), u_off // conv_ch)),
        pl.BlockSpec((HALO, conv_ch), lambda i: (halo_rows(i), u_off // conv_ch + 2)),
        pl.BlockSpec((tm, sb_width), lambda i: (i, 0)),
        resident(pool_w.shape),
        resident(pool_scale.shape),
        resident(conv_w.shape),
        resident(w_a.shape),
        resident(w_b.shape),
        resident(w_c.shape),
        pl.BlockSpec((tm, d), lambda i: (i, 0)),
        pl.BlockSpec((tm, d), lambda i: (i, 1)),
        pl.BlockSpec((tm, d), lambda i: (i, 2)),
    ]
    return pl.pallas_call(
        functools.partial(_mix_kernel, per_seq=per_seq),
        grid=(n // tm,),
        in_specs=in_specs,
        out_specs=pl.BlockSpec((tm, d), lambda i: (i, 0)),
        out_shape=jax.ShapeDtypeStruct((n, d), BF16),
        scratch_shapes=[pltpu.VMEM((tm, pool_width), BF16), pltpu.VMEM((tm, conv_ch), BF16)],
        compiler_params=_params("parallel"),
        name="mix_merge",
    )(pg, pg, pg, pg, pg, pg, yb, pool_w, pool_scale, conv_w, w_a, w_b, w_c,
      gates, gates, gates)


def _out_final_kernel(m_ref, w_ref, x_ref, rg_ref, fg_ref, o_ref):
    y = jnp.dot(m_ref[...], w_ref[...], preferred_element_type=F32)
    xn = x_ref[...] + rg_ref[0] * y
    ms = jnp.mean(xn * xn, axis=-1, keepdims=True)
    o_ref[...] = xn * lax.rsqrt(ms + RMS_EPS) * fg_ref[...]


def _out_next_kernel(m_ref, w_ref, x_ref, rg_ref, g_ref, sc_ref, sh_ref, o_ref, h_ref):
    y = jnp.dot(m_ref[...], w_ref[...], preferred_element_type=F32)
    xn = x_ref[...] + rg_ref[0] * y
    o_ref[...] = xn
    h_ref[...] = _modulated_norm(xn, g_ref, sc_ref, sh_ref)


def _out(merged, w_out, layer, x2, res_gate3, seq, tm, final_g=None, next_norm=None):
    n, d = x2.shape
    per_seq = seq // tm
    rows = pl.BlockSpec((tm, d), lambda i: (i, 0))
    vec = pl.BlockSpec((1, d), lambda i: (0, 0))
    per_batch = pl.BlockSpec((1, 1, d), lambda i: (i // per_seq, 0, 0))
    in_specs = [rows, pl.BlockSpec((None, d, d), lambda i: (layer, 0, 0)), rows, per_batch]
    args = [merged, w_out, x2, res_gate3]
    if final_g is not None:
        return pl.pallas_call(
            _out_final_kernel,
            grid=(n // tm,),
            in_specs=in_specs + [vec],
            out_specs=rows,
            out_shape=jax.ShapeDtypeStruct((n, d), F32),
            compiler_params=_params("parallel"),
            name="out_proj_final",
        )(*args, final_g)
    return pl.pallas_call(
        _out_next_kernel,
        grid=(n // tm,),
        in_specs=in_specs + [vec, per_batch, per_batch],
        out_specs=[rows, rows],
        out_shape=[jax.ShapeDtypeStruct((n, d), F32), jax.ShapeDtypeStruct((n, d), BF16)],
        compiler_params=_params("parallel"),
        name="out_proj",
    )(*args, *next_norm)


def kernel(x, c, norm_g, w_ada, b_ada, w_in, pool_w, pool_scale, conv_w, w_br_a, w_br_b,
           w_br_c, w_gate, b_gate, w_out, final_g):
    batch, seq, d = x.shape
    depth = w_in.shape[0]
    pool_width = w_br_a.shape[1]
    sb_width = w_br_b.shape[1]
    q_off = 2 * pool_width
    n = batch * seq
    tm = min(512, seq)

    c_pad = jnp.pad(c, ((0, -batch % 8), (0, 0)))
    mod = _ada(c_pad, w_ada, b_ada)[:, :batch]
    x2 = x.reshape(n, d)

    def norm_inputs(l):
        return (norm_g[l].reshape(1, d), mod[l, :, d:2 * d].reshape(batch, 1, d),
                mod[l, :, :d].reshape(batch, 1, d))

    in_dim = w_in.shape[2]
    q_cols = (jnp.arange(in_dim) >= q_off) & (jnp.arange(in_dim) < q_off + sb_width)
    in_scale = jnp.where(q_cols, HEAD_DIM ** -0.5, 1.0).astype(F32).reshape(1, in_dim)
    gate_scale = jnp.ones((1, w_gate.shape[2]), F32)

    mix_params = (pool_w.astype(BF16), pool_scale.reshape(depth, 1, pool_width), conv_w,
                  w_br_a.astype(BF16), w_br_b.astype(BF16), w_br_c.astype(BF16))
    w_out_bf = w_out.astype(BF16)

    h = _norm_mod(x2, *norm_inputs(0), seq, tm)
    for l in range(depth):
        res_gate3 = mod[l, :, 2 * d:].reshape(batch, 1, d)
        pg = _proj(h, w_in, l, in_scale, None, tm, PROJ_TN)
        gates = _proj(h, w_gate, l, gate_scale, b_gate[l].reshape(1, -1), tm, PROJ_TN)
        yb = _attention(pg, batch, seq, sb_width, q_off)
        merged = _mix(pg, gates, yb, l, *mix_params, seq, min(256, seq))
        if l == depth - 1:
            x2 = _out(merged, w_out_bf, l, x2, res_gate3, seq, tm,
                      final_g=final_g.reshape(1, d))
        else:
            x2, h = _out(merged, w_out_bf, l, x2, res_gate3, seq, tm,
                         next_norm=norm_inputs(l + 1))

    return x2.reshape(batch, seq, d)
```

```python
import functools

import jax
import jax.numpy as jnp
from jax import lax
from jax.experimental import pallas as pl
from jax.experimental.pallas import tpu as pltpu

RMS_EPS = 1e-6
HEAD_DIM = 128
POOL_WINDOWS = (2, 4, 8, 16)
CONV_WIDTH = 3
HALO = 16

VMEM_LIMIT_BYTES = 56 * 1024 * 1024
ROW_TILE = 512
PROJ_TN = 2048
MIX_ROW_TILE = 256
ADA_TN = 1536
ATTN_ROWS = 128
ATTN_UNROLL = 8
ATTN_WAVE = 4

F32 = jnp.float32
BF16 = jnp.bfloat16


def _params(*sem):
    return pltpu.CompilerParams(dimension_semantics=sem, vmem_limit_bytes=VMEM_LIMIT_BYTES)


def _sigmoid(v):
    return 0.5 * jnp.tanh(0.5 * v) + 0.5


def _silu(v):
    return v * _sigmoid(v)


def _ada_kernel(c_ref, w_ref, b_ref, o_ref):
    s = _silu(c_ref[...])
    o_ref[0] = jnp.dot(s, w_ref[0], precision=lax.Precision.HIGHEST,
                       preferred_element_type=F32) + b_ref[0]


def _ada(c_pad, w_ada, b_ada):
    depth, d, n3 = w_ada.shape
    tn = ADA_TN
    rows = c_pad.shape[0]
    return pl.pallas_call(
        _ada_kernel,
        grid=(depth, n3 // tn),
        in_specs=[
            pl.BlockSpec((rows, d), lambda l, j: (0, 0)),
            pl.BlockSpec((1, d, tn), lambda l, j: (l, 0, j)),
            pl.BlockSpec((1, 1, tn), lambda l, j: (l, 0, j)),
        ],
        out_specs=pl.BlockSpec((1, rows, tn), lambda l, j: (l, 0, j)),
        out_shape=jax.ShapeDtypeStruct((depth, rows, n3), F32),
        compiler_params=_params("parallel", "parallel"),
        name="ada_mod",
    )(c_pad, w_ada, b_ada.reshape(depth, 1, n3))


CAST_ROWS = 64


def _modulated_norm(x, g_ref, sc_ref, sh_ref):
    ms = jnp.mean(x * x, axis=-1, keepdims=True)
    y = x * lax.rsqrt(ms + RMS_EPS) * g_ref[...]
    return (y * (1.0 + sc_ref[0]) + sh_ref[0]).astype(BF16)


def _norm_mod_kernel(x_ref, g_ref, sc_ref, sh_ref, h_ref):
    h_ref[...] = _modulated_norm(x_ref[...], g_ref, sc_ref, sh_ref)


def _norm_mod(x2, g, scale3, shift3, seq, tm):
    n, d = x2.shape
    per_seq = seq // tm
    return pl.pallas_call(
        _norm_mod_kernel,
        grid=(n // tm,),
        in_specs=[
            pl.BlockSpec((tm, d), lambda i: (i, 0)),
            pl.BlockSpec((1, d), lambda i: (0, 0)),
            pl.BlockSpec((1, 1, d), lambda i: (i // per_seq, 0, 0)),
            pl.BlockSpec((1, 1, d), lambda i: (i // per_seq, 0, 0)),
        ],
        out_specs=pl.BlockSpec((tm, d), lambda i: (i, 0)),
        out_shape=jax.ShapeDtypeStruct((n, d), BF16),
        compiler_params=_params("parallel"),
        name="norm_mod",
    )(x2, g, scale3, shift3)


def _proj_kernel(h_ref, w_ref, vec_ref, o_ref, wb_ref, *, gated):
    @pl.when(pl.program_id(1) == 0)
    def _():
        def body(r, carry):
            rows = pl.ds(pl.multiple_of(r * CAST_ROWS, CAST_ROWS), CAST_ROWS)
            w = w_ref[rows, :]
            wb_ref[rows, :] = (w if gated else w * vec_ref[...]).astype(BF16)
            return carry

        lax.fori_loop(0, w_ref.shape[0] // CAST_ROWS, body, 0)

    acc = jnp.dot(h_ref[...], wb_ref[...], preferred_element_type=F32)
    if gated:
        acc = _sigmoid(acc + vec_ref[...])
    o_ref[...] = acc.astype(o_ref.dtype)


def _proj(h, w_stack, layer, vec, gated):
    n, d = h.shape
    ncol = w_stack.shape[2]
    tm, tn = ROW_TILE, PROJ_TN
    return pl.pallas_call(
        functools.partial(_proj_kernel, gated=gated),
        grid=(ncol // tn, n // tm),
        in_specs=[
            pl.BlockSpec((tm, d), lambda j, i: (i, 0)),
            pl.BlockSpec((None, d, tn), lambda j, i: (layer, 0, j)),
            pl.BlockSpec((1, tn), lambda j, i: (0, j)),
        ],
        out_specs=pl.BlockSpec((tm, tn), lambda j, i: (i, j)),
        out_shape=jax.ShapeDtypeStruct((n, ncol), BF16),
        scratch_shapes=[pltpu.VMEM((d, tn), BF16)],
        compiler_params=_params("arbitrary", "arbitrary"),
        name="gate_proj" if gated else "in_proj",
    )(h, w_stack, vec)


LOG2E = 1.4426950408889634
DEAD_CARRY = 88.0
MASKED_LOGIT = -1e30


def _attn_kernel(q_ref, k_ref, v_ref, zb_ref, o_ref, *, rows, unroll, wave):
    seq = q_ref.shape[0]
    win = 2 * rows
    col = lax.broadcasted_iota(jnp.int32, (rows, win), 1)
    ahead = col - lax.broadcasted_iota(jnp.int32, (rows, win), 0)
    suffix = (lax.broadcasted_iota(jnp.int32, (win, win), 0)
              > lax.broadcasted_iota(jnp.int32, (win, win), 1)).astype(BF16)

    def chunk_start(top):
        return pl.multiple_of(jnp.maximum(top - win, 0), rows)

    def scores(q, start, visible):
        z = lax.dot_general(q, k_ref[pl.ds(start, win), :], (((1,), (1,)), ((), ())),
                            preferred_element_type=F32)
        if visible is not None:
            z = jnp.where(visible, z, MASKED_LOGIT)
        return z, jnp.maximum(z, 0.0) + jnp.log(1.0 + jnp.exp2(jnp.abs(z) * -LOG2E))

    def later_sums(sp):
        return jnp.dot(sp.astype(BF16), suffix, preferred_element_type=F32)

    def weighted(z, sp, between, start):
        a = jnp.exp2((z - sp - between) * LOG2E)
        return jnp.dot(a.astype(BF16), v_ref[pl.ds(start, win), :], preferred_element_type=F32)

    def row_sums(sp, between):
        return between[:, 0:1] + sp[:, 0:1]

    def group_rows(g):
        return pl.ds(pl.multiple_of(g * rows, rows), rows)

    def step(it, _, near_start):
        groups = [it * unroll + u for u in range(unroll)]
        qs = [q_ref[group_rows(g), :] for g in groups]
        starts1 = [chunk_start((g + 1) * rows) for g in groups]
        starts2 = [chunk_start(s1) for s1 in starts1]
        starts = starts1 + starts2
        if near_start:
            visible = ([ahead < g * rows - s1 for g, s1 in zip(groups, starts1)]
                       + [col < s1 - s2 for s1, s2 in zip(starts1, starts2)])
        else:
            visible = [ahead < rows] * unroll + [None] * unroll
        n_chunks = 2 * unroll
        zs, betweens, pvs = [None] * n_chunks, [None] * n_chunks, [None] * n_chunks
        for w in range(n_chunks // wave + 2):
            for c in range(w * wave, min((w + 1) * wave, n_chunks)):
                zs[c] = scores(qs[c % unroll], starts[c], visible[c])
            for c in range(max(w - 1, 0) * wave, min(w * wave, n_chunks)):
                betweens[c] = later_sums(zs[c][1])
            for c in range(max(w - 2, 0) * wave, min((w - 1) * wave, n_chunks)):
                pvs[c] = weighted(zs[c][0], zs[c][1], betweens[c], starts[c])
        sums = [row_sums(sp, b) for (_, sp), b in zip(zs, betweens)]

        accs = [pvs[u] + jnp.exp(-sums[u]) * pvs[unroll + u] for u in range(unroll)]
        carries = [sums[u] + sums[unroll + u] for u in range(unroll)]
        alive = [jnp.min(carry) < DEAD_CARRY for carry in carries]

        for u, g in enumerate(groups):
            def sweep_earlier(q=qs[u], acc=accs[u], carry=carries[u], top=starts2[u]):
                def live(state):
                    top, carry, _ = state
                    return jnp.logical_and(top > 0, jnp.min(carry) < DEAD_CARRY)

                def earlier_chunk(state):
                    top, carry, acc = state
                    start = chunk_start(top)
                    z, sp = scores(q, start, col < top - start)
                    between = later_sums(sp)
                    pv = weighted(z, sp, between, start)
                    return start, carry + row_sums(sp, between), acc + jnp.exp(-carry) * pv

                return lax.while_loop(live, earlier_chunk, (top, carry, acc))[2]

            acc = lax.cond(alive[u], sweep_earlier, lambda acc=accs[u]: acc)
            zb = zb_ref[group_rows(g), :].astype(F32)
            o_ref[group_rows(g), :] = (acc * _silu(zb)).astype(o_ref.dtype)
        return 0

    assert unroll >= 3
    step(0, 0, True)
    lax.fori_loop(1, seq // (rows * unroll), functools.partial(step, near_start=False), 0)


def _attention(pg, batch, seq, sb_width, q_off, rows=ATTN_ROWS, unroll=ATTN_UNROLL,
               wave=ATTN_WAVE):
    n = pg.shape[0]
    heads = sb_width // HEAD_DIM
    c0 = q_off // HEAD_DIM
    assert seq % (rows * unroll) == 0 and seq >= 2 * rows

    def spec(section):
        return pl.BlockSpec((seq, HEAD_DIM), lambda b, h: (b, c0 + section * heads + h))

    return pl.pallas_call(
        functools.partial(_attn_kernel, rows=rows, unroll=unroll, wave=wave),
        grid=(batch, heads),
        in_specs=[spec(0), spec(1), spec(2), spec(3)],
        out_specs=pl.BlockSpec((seq, HEAD_DIM), lambda b, h: (b, h)),
        out_shape=jax.ShapeDtypeStruct((n, sb_width), BF16),
        compiler_params=_params("parallel", "parallel"),
        name="stickbreak_attn",
    )(pg, pg, pg, pg)


def _mix_kernel(xz_ref, ub_ref, cz_ref, xa_h_ref, u_h_ref, cg_h_ref, yb_ref,
                pw_ref, ps_ref, cw_ref, wa_ref, wb_ref, wc_ref, g0_ref, g1_ref, g2_ref,
                o_ref, ya_ref, yc_ref, *, per_seq):
    tm = xz_ref.shape[0]
    pool_width = xz_ref.shape[1] // 2
    conv_ch = ub_ref.shape[1] // 2
    group = pool_width // len(POOL_WINDOWS)
    i = pl.program_id(0)

    seq_start = (i % per_seq) == 0
    keep = jnp.where(seq_start, 0.0, 1.0)
    pos = (i % per_seq) * tm + lax.broadcasted_iota(jnp.int32, (tm, 1), 0)

    def pool_group(g, w):
        cs = slice(g * group, (g + 1) * group)
        xa = xz_ref[:, cs].astype(F32)
        halo = xa_h_ref[:, cs].astype(F32) * keep
        s = jnp.concatenate([halo, xa], axis=0)
        shift = 1
        while shift < w:
            s = s + pltpu.roll(s, shift, axis=0)
            shift *= 2
        count = jnp.minimum(pos + 1, w).astype(F32)
        mixed = s[HALO:, :] / count - xa
        y = jnp.dot(mixed.astype(BF16), pw_ref[g], preferred_element_type=F32)
        za = xz_ref[:, pool_width + g * group: pool_width + (g + 1) * group].astype(F32)
        ya_ref[:, cs] = (y * ps_ref[:, cs] * _silu(za)).astype(BF16)

    def conv_channels(cs):
        gate_cs = slice(conv_ch + cs.start, conv_ch + cs.stop)
        v = cz_ref[:, cs].astype(F32) * ub_ref[:, cs].astype(F32)
        vh = cg_h_ref[:, cs].astype(F32) * u_h_ref[:, cs].astype(F32) * keep
        ve = jnp.concatenate([vh, v], axis=0)
        y = cw_ref[CONV_WIDTH - 1: CONV_WIDTH, cs] * v
        for tap in range(CONV_WIDTH - 1):
            back = CONV_WIDTH - 1 - tap
            y = y + cw_ref[tap: tap + 1, cs] * pltpu.roll(ve, back, axis=0)[HALO:, :]
        yc_ref[:, cs] = (ub_ref[:, gate_cs].astype(F32) * y
                         * _silu(cz_ref[:, gate_cs].astype(F32))).astype(BF16)

    pieces = [functools.partial(pool_group, g, w) for g, w in enumerate(POOL_WINDOWS)]
    pieces += [functools.partial(conv_channels, slice(k * group, (k + 1) * group))
               for k in range(conv_ch // group)]
    slab = wb_ref.shape[1] // len(pieces)
    b_slabs = []
    for k, piece in enumerate(pieces):
        b_slabs.append(jnp.dot(yb_ref[...], wb_ref[:, k * slab:(k + 1) * slab],
                               preferred_element_type=F32))
        piece()
    b = jnp.concatenate(b_slabs, axis=1)
    a = jnp.dot(ya_ref[...], wa_ref[...], preferred_element_type=F32)
    c = jnp.dot(yc_ref[...], wc_ref[...], preferred_element_type=F32)
    merged = (g0_ref[...].astype(F32) * a + g1_ref[...].astype(F32) * b
              + g2_ref[...].astype(F32) * c)
    o_ref[...] = merged.astype(o_ref.dtype)


def _mix(pg, gates, yb, layer, pool_w, pool_scale, conv_w, w_a, w_b, w_c, seq, tm):
    n = pg.shape[0]
    _, pool_width, d = w_a.shape
    sb_width = w_b.shape[1]
    conv_ch = w_c.shape[1]
    per_seq = seq // tm
    u_off = 2 * pool_width + 4 * sb_width
    wide = 2 * pool_width
    assert 2 * conv_ch == wide and u_off % wide == 0

    def halo_rows(i):
        return jnp.maximum(i * (tm // HALO) - 1, 0)

    def resident(stack_shape):
        tail = stack_shape[1:]
        return pl.BlockSpec((None,) + tail, lambda i: (layer,) + (0,) * len(tail),
                            pipeline_mode=pl.Buffered(1))

    in_specs = [
        pl.BlockSpec((tm, wide), lambda i: (i, 0)),
        pl.BlockSpec((tm, wide), lambda i: (i, u_off // wide)),
        pl.BlockSpec((tm, wide), lambda i: (i, u_off // wide + 1)),
        pl.BlockSpec((HALO, pool_width), lambda i: (halo_rows(i), 0)),
        pl.BlockSpec((HALO, conv_ch), lambda i: (halo_rows(i), u_off // conv_ch)),
        pl.BlockSpec((HALO, conv_ch), lambda i: (halo_rows(i), u_off // conv_ch + 2)),
        pl.BlockSpec((tm, sb_width), lambda i: (i, 0)),
        resident(pool_w.shape),
        resident(pool_scale.shape),
        resident(conv_w.shape),
        resident(w_a.shape),
        resident(w_b.shape),
        resident(w_c.shape),
        pl.BlockSpec((tm, d), lambda i: (i, 0)),
        pl.BlockSpec((tm, d), lambda i: (i, 1)),
        pl.BlockSpec((tm, d), lambda i: (i, 2)),
    ]
    return pl.pallas_call(
        functools.partial(_mix_kernel, per_seq=per_seq),
        grid=(n // tm,),
        in_specs=in_specs,
        out_specs=pl.BlockSpec((tm, d), lambda i: (i, 0)),
        out_shape=jax.ShapeDtypeStruct((n, d), BF16),
        scratch_shapes=[pltpu.VMEM((tm, pool_width), BF16), pltpu.VMEM((tm, conv_ch), BF16)],
        compiler_params=_params("parallel"),
        name="mix_merge",
    )(pg, pg, pg, pg, pg, pg, yb, pool_w, pool_scale, conv_w, w_a, w_b, w_c,
      gates, gates, gates)


def _out_final_kernel(m_ref, w_ref, x_ref, rg_ref, fg_ref, o_ref):
    y = jnp.dot(m_ref[...], w_ref[...], preferred_element_type=F32)
    xn = x_ref[...] + rg_ref[0] * y
    ms = jnp.mean(xn * xn, axis=-1, keepdims=True)
    o_ref[...] = xn * lax.rsqrt(ms + RMS_EPS) * fg_ref[...]


def _out_next_kernel(m_ref, w_ref, x_ref, rg_ref, g_ref, sc_ref, sh_ref, o_ref, h_ref):
    y = jnp.dot(m_ref[...], w_ref[...], preferred_element_type=F32)
    xn = x_ref[...] + rg_ref[0] * y
    o_ref[...] = xn
    h_ref[...] = _modulated_norm(xn, g_ref, sc_ref, sh_ref)


def _out(merged, w_out, layer, x2, res_gate3, seq, tm, final_g=None, next_norm=None):
    n, d = x2.shape
    per_seq = seq // tm
    rows = pl.BlockSpec((tm, d), lambda i: (i, 0))
    vec = pl.BlockSpec((1, d), lambda i: (0, 0))
    per_batch = pl.BlockSpec((1, 1, d), lambda i: (i // per_seq, 0, 0))
    in_specs = [rows, pl.BlockSpec((None, d, d), lambda i: (layer, 0, 0)), rows, per_batch]
    args = [merged, w_out, x2, res_gate3]
    if final_g is not None:
        return pl.pallas_call(
            _out_final_kernel,
            grid=(n // tm,),
            in_specs=in_specs + [vec],
            out_specs=rows,
            out_shape=jax.ShapeDtypeStruct((n, d), F32),
            compiler_params=_params("parallel"),
            name="out_proj_final",
        )(*args, final_g)
    return pl.pallas_call(
        _out_next_kernel,
        grid=(n // tm,),
        in_specs=in_specs + [vec, per_batch, per_batch],
        out_specs=[rows, rows],
        out_shape=[jax.ShapeDtypeStruct((n, d), F32), jax.ShapeDtypeStruct((n, d), BF16)],
        compiler_params=_params("parallel"),
        name="out_proj",
    )(*args, *next_norm)


def kernel(x, c, norm_g, w_ada, b_ada, w_in, pool_w, pool_scale, conv_w, w_br_a, w_br_b,
           w_br_c, w_gate, b_gate, w_out, final_g):
    batch, seq, d = x.shape
    depth = w_in.shape[0]
    pool_width = w_br_a.shape[1]
    sb_width = w_br_b.shape[1]
    q_off = 2 * pool_width
    n = batch * seq
    tm = ROW_TILE
    assert seq % ROW_TILE == 0 and seq % MIX_ROW_TILE == 0

    c_pad = jnp.pad(c, ((0, -batch % 8), (0, 0)))
    mod = _ada(c_pad, w_ada, b_ada)[:, :batch]
    x2 = x.reshape(n, d)

    def norm_inputs(l):
        return (norm_g[l].reshape(1, d), mod[l, :, d:2 * d].reshape(batch, 1, d),
                mod[l, :, :d].reshape(batch, 1, d))

    in_dim = w_in.shape[2]
    q_cols = (jnp.arange(in_dim) >= q_off) & (jnp.arange(in_dim) < q_off + sb_width)
    in_scale = jnp.where(q_cols, HEAD_DIM ** -0.5, 1.0).astype(F32).reshape(1, in_dim)

    mix_params = (pool_w.astype(BF16), pool_scale.reshape(depth, 1, pool_width), conv_w,
                  w_br_a.astype(BF16), w_br_b.astype(BF16), w_br_c.astype(BF16))
    w_out_bf = w_out.astype(BF16)

    h = _norm_mod(x2, *norm_inputs(0), seq, tm)
    for l in range(depth):
        res_gate3 = mod[l, :, 2 * d:].reshape(batch, 1, d)
        pg = _proj(h, w_in, l, in_scale, gated=False)
        gates = _proj(h, w_gate, l, b_gate[l].reshape(1, -1), gated=True)
        yb = _attention(pg, batch, seq, sb_width, q_off)
        merged = _mix(pg, gates, yb, l, *mix_params, seq, MIX_ROW_TILE)
        if l == depth - 1:
            x2 = _out(merged, w_out_bf, l, x2, res_gate3, seq, tm,
                      final_g=final_g.reshape(1, d))
        else:
            x2, h = _out(merged, w_out_bf, l, x2, res_gate3, seq, tm,
                         next_norm=norm_inputs(l + 1))

    return x2.reshape(batch, seq, d)
```
